```python
import math
import jax, jax.numpy as jnp
from jax import lax
import numpy as np

D_MODEL = 2048
BATCH = 1
SEQ = 8192
DEPTH = 2
DEC_BATCH = 32
DEC_SEQ = 1
PAST_LEN = 8192
PAGE_SIZE = 128

H_RET = 4
DK_RET = D_MODEL // (2 * H_RET)
DV_RET = D_MODEL // (2 * H_RET)
RET_WIDTH = H_RET * DV_RET
RET_CHUNK = 128
H_DIFF = 4
DH_DIFF = D_MODEL // (4 * H_DIFF)
DV_DIFF = 2 * DH_DIFF
DIFF_WIDTH = H_DIFF * DV_DIFF
MIX_WIDTH = RET_WIDTH + DIFF_WIDTH
Q_BLOCK = 128
IN_SIZES = (H_RET * DK_RET, H_RET * DK_RET, RET_WIDTH, RET_WIDTH,
            2 * H_DIFF * DH_DIFF, 2 * H_DIFF * DH_DIFF, DIFF_WIDTH)
W_IN = sum(IN_SIZES)
D_FF = -(-8 * D_MODEL // (3 * 256)) * 256
N_MOD = 6
EPS = 1e-6

kernel_name = "hymba_retention_diffattn_adaln_step"


def _rms_norm(x, g):
    xf = x.astype(jnp.float32)
    y = xf * lax.rsqrt(jnp.mean(xf * xf, axis=-1, keepdims=True) + EPS)
    return (y * g.astype(jnp.float32)).astype(x.dtype)


def _log_gamma():
    return jnp.log1p(-jnp.exp2(-5.0 - jnp.arange(H_RET, dtype=jnp.float32)))


def _alibi_slopes():
    return jnp.exp2(-8.0 * (jnp.arange(H_DIFF, dtype=jnp.float32) + 1.0) / H_DIFF)


def _adaln(c, w_ada, b_ada):
    mod = jax.nn.silu(c) @ w_ada + b_ada
    return jnp.split(mod[:, None, :], N_MOD, axis=-1)


def _mixer_in(x, shift, scale, g_norm, w_in, g_q, g_k):
    h = _rms_norm(x, g_norm) * (1.0 + scale) + shift
    B, T, _ = h.shape
    pts = tuple(int(p) for p in np.cumsum(IN_SIZES)[:-1])
    q_r, k_r, v_r, g_r, q_d, k_d, v_d = jnp.split(h @ w_in, pts, axis=-1)
    q_r = q_r.reshape(B, T, H_RET, DK_RET)
    k_r = k_r.reshape(B, T, H_RET, DK_RET) * (DK_RET ** -0.5)
    v_r = v_r.reshape(B, T, H_RET, DV_RET)
    q_d = _rms_norm(q_d.reshape(B, T, H_DIFF, 2, DH_DIFF), g_q)
    k_d = _rms_norm(k_d.reshape(B, T, H_DIFF, 2, DH_DIFF), g_k)
    v_d = v_d.reshape(B, T, H_DIFF, DV_DIFF)
    return q_r, k_r, v_r, g_r, q_d, k_d, v_d


def _retention_chunk(S, q, k, v, log_gamma):
    C = q.shape[1]
    pos = jnp.arange(C, dtype=jnp.float32)
    diff = pos[:, None] - pos[None, :]
    decay = jnp.where(diff >= 0,
                      jnp.exp(jnp.maximum(diff, 0.0)[None] * log_gamma[:, None, None]), 0.0)
    scores = jnp.einsum('bihd,bjhd->bhij', q, k) * decay[None]
    inner = jnp.einsum('bhij,bjhv->bihv', scores, v)
    q_decay = jnp.exp((pos + 1.0)[:, None] * log_gamma[None, :])
    cross = jnp.einsum('bihd,bhdv->bihv', q, S) * q_decay[None, :, :, None]
    k_decay = jnp.exp((C - 1.0 - pos)[:, None] * log_gamma[None, :])
    S_new = (jnp.exp(C * log_gamma)[None, :, None, None] * S
             + jnp.einsum('bjhd,bjhv->bhdv', k * k_decay[None, :, :, None], v))
    return S_new, inner + cross


def _retention_prompt(q, k, v, log_gamma):
    B, T, H, _ = q.shape
    nc = T // RET_CHUNK

    def to_chunks(a):
        return jnp.moveaxis(a.astype(jnp.float32).reshape(B, nc, RET_CHUNK, H, a.shape[-1]), 1, 0)

    S0 = jnp.zeros((B, H, DK_RET, DV_RET), jnp.float32)

    def step(S, qkv):
        return _retention_chunk(S, qkv[0], qkv[1], qkv[2], log_gamma)

    S_fin, o = lax.scan(step, S0, (to_chunks(q), to_chunks(k), to_chunks(v)))
    o = jnp.moveaxis(o, 0, 1).reshape(B, T, H, DV_RET)
    return S_fin, o.astype(v.dtype)


def _diff_logits(q, k, qpos, kpos):
    dist = (qpos[:, None] - kpos[None, :]).astype(jnp.float32)
    bias = jnp.where(dist >= 0, -_alibi_slopes()[:, None, None] * dist, -jnp.inf)
    s = jnp.einsum('bqhmd,bkhmd->bmhqk', q.astype(jnp.float32), k.astype(jnp.float32)) * (DH_DIFF ** -0.5)
    return s + bias[None, None]


def _diff_weights(logits, lam):
    a = jax.nn.softmax(logits, axis=-1)
    return a[:, 0] - lam * a[:, 1]


def _diff_attn_prompt(q, k, v, lam):
    B, T = q.shape[:2]
    nb = T // Q_BLOCK
    kpos = jnp.arange(T)
    qb = jnp.moveaxis(q.reshape(B, nb, Q_BLOCK, H_DIFF, 2, DH_DIFF), 1, 0)
    qposb = kpos.reshape(nb, Q_BLOCK)

    def block(args):
        qi, qpos = args
        w = _diff_weights(_diff_logits(qi, k, qpos, kpos), lam)
        return jnp.einsum('bhqk,bkhv->bqhv', w.astype(v.dtype), v)

    o = lax.map(block, (qb, qposb))
    return jnp.moveaxis(o, 0, 1).reshape(B, T, H_DIFF, DV_DIFF)


def _diff_attn_sample(q, k_new, v_new, k_past, v_past, lam):
    P = k_past.shape[1]
    Tq = q.shape[1]
    qpos = P + jnp.arange(Tq)
    logits = jnp.concatenate([_diff_logits(q, k_past, qpos, jnp.arange(P)),
                              _diff_logits(q, k_new, qpos, qpos)], axis=-1)
    w = _diff_weights(logits, lam)
    return (jnp.einsum('bhqk,bkhv->bqhv', w[..., :P].astype(v_past.dtype), v_past)
            + jnp.einsum('bhqk,bkhv->bqhv', w[..., P:].astype(v_new.dtype), v_new))


def _mixer_out(o_r, g_r, o_d, g_ret_norm, g_diff_norm, lam_init, w_out):
    B, T = o_r.shape[:2]
    r = _rms_norm(o_r, g_ret_norm).reshape(B, T, RET_WIDTH) * jax.nn.silu(g_r)
    d = (_rms_norm(o_d, g_diff_norm) * (1.0 - lam_init)).reshape(B, T, DIFF_WIDTH)
    return jnp.concatenate([r, d], axis=-1) @ w_out


def _ffn(x, shift, scale, g_norm, w_gate, w_up, w_down):
    h = _rms_norm(x, g_norm) * (1.0 + scale) + shift
    return (jax.nn.silu(h @ w_gate) * (h @ w_up)) @ w_down


def setup_inputs(seed: int = 0) -> dict:
    key = jax.random.key(seed)
    ks = iter(jax.random.split(key, 40))

    def nrm(shape, scale):
        return jax.random.normal(next(ks), shape, jnp.float32) * scale

    n_pages = PAST_LEN // PAGE_SIZE
    n_used = DEC_BATCH * n_pages
    n_pool = n_used + (n_used + 3) // 4
    page_table = jax.random.permutation(next(ks), n_pool)[:n_used].reshape(DEC_BATCH, n_pages).astype(jnp.int32)
    return {
        "x_prompt": nrm((BATCH, SEQ, D_MODEL), 1.0),
        "x_sample": nrm((DEC_BATCH, DEC_SEQ, D_MODEL), 1.0),
        "cache_k": nrm((DEPTH, n_pool, PAGE_SIZE, H_DIFF, 2, DH_DIFF), 1.0),
        "cache_v": nrm((DEPTH, n_pool, PAGE_SIZE, H_DIFF, DV_DIFF), 1.0),
        "state_ret": nrm((DEPTH, DEC_BATCH, H_RET, DK_RET, DV_RET), 0.1),
        "page_table": page_table,
        "c_prompt": nrm((BATCH, D_MODEL), 1.0),
        "c_sample": nrm((DEC_BATCH, D_MODEL), 1.0),
        "w_ada": nrm((DEPTH, D_MODEL, N_MOD * D_MODEL), 0.5 * D_MODEL ** -0.5),
        "b_ada": nrm((DEPTH, N_MOD * D_MODEL), 0.02),
        "g_norm_mix": 1.0 + nrm((DEPTH, D_MODEL), 0.02),
        "w_in": nrm((DEPTH, D_MODEL, W_IN), D_MODEL ** -0.5),
        "g_qnorm": 1.0 + nrm((DEPTH, DH_DIFF), 0.02),
        "g_knorm": 1.0 + nrm((DEPTH, DH_DIFF), 0.02),
        "lambda_q1": nrm((DEPTH, DH_DIFF), 0.1),
        "lambda_k1": nrm((DEPTH, DH_DIFF), 0.1),
        "lambda_q2": nrm((DEPTH, DH_DIFF), 0.1),
        "lambda_k2": nrm((DEPTH, DH_DIFF), 0.1),
        "g_ret_norm": 1.0 + nrm((DEPTH, H_RET, DV_RET), 0.02),
        "g_diff_norm": 1.0 + nrm((DEPTH, DV_DIFF), 0.02),
        "w_out": nrm((DEPTH, MIX_WIDTH, D_MODEL), MIX_WIDTH ** -0.5),
        "g_norm_ffn": 1.0 + nrm((DEPTH, D_MODEL), 0.02),
        "w_gate": nrm((DEPTH, D_MODEL, D_FF), D_MODEL ** -0.5),
        "w_up": nrm((DEPTH, D_MODEL, D_FF), D_MODEL ** -0.5),
        "w_down": nrm((DEPTH, D_FF, D_MODEL), D_FF ** -0.5),
    }


def reference(x_prompt, x_sample, cache_k, cache_v, state_ret, page_table, c_prompt, c_sample,
              w_ada, b_ada, g_norm_mix, w_in, g_qnorm, g_knorm, lambda_q1, lambda_k1, lambda_q2,
              lambda_k2, g_ret_norm, g_diff_norm, w_out, g_norm_ffn, w_gate, w_up, w_down):
    log_gamma = _log_gamma()
    x, xs = x_prompt, x_sample
    Bs = x_sample.shape[0]
    k_p, v_p, S_p, k_s, v_s, S_s = [], [], [], [], [], []
    for l in range(DEPTH):
        lam_init = 0.8 - 0.6 * math.exp(-0.3 * l)
        lam = (jnp.exp(jnp.sum(lambda_q1[l].astype(jnp.float32) * lambda_k1[l].astype(jnp.float32)))
               - jnp.exp(jnp.sum(lambda_q2[l].astype(jnp.float32) * lambda_k2[l].astype(jnp.float32)))
               + lam_init)

        sh_a, sc_a, gt_a, sh_f, sc_f, gt_f = _adaln(c_prompt, w_ada[l], b_ada[l])
        q_r, k_r, v_r, g_r, q_d, k_d, v_d = _mixer_in(x, sh_a, sc_a, g_norm_mix[l], w_in[l], g_qnorm[l], g_knorm[l])
        S_fin, o_r = _retention_prompt(q_r, k_r, v_r, log_gamma)
        o_d = _diff_attn_prompt(q_d, k_d, v_d, lam)
        x = x + gt_a * _mixer_out(o_r, g_r, o_d, g_ret_norm[l], g_diff_norm[l], lam_init, w_out[l])
        x = x + gt_f * _ffn(x, sh_f, sc_f, g_norm_ffn[l], w_gate[l], w_up[l], w_down[l])
        k_p.append(k_d)
        v_p.append(v_d)
        S_p.append(S_fin)

        sh_a, sc_a, gt_a, sh_f, sc_f, gt_f = _adaln(c_sample, w_ada[l], b_ada[l])
        q_r, k_r, v_r, g_r, q_d, k_d, v_d = _mixer_in(xs, sh_a, sc_a, g_norm_mix[l], w_in[l], g_qnorm[l], g_knorm[l])
        S_new, o_r = _retention_chunk(state_ret[l].astype(jnp.float32), q_r.astype(jnp.float32),
                                      k_r.astype(jnp.float32), v_r.astype(jnp.float32), log_gamma)
        o_r = o_r.astype(v_r.dtype)
        k_past = cache_k[l, page_table].reshape(Bs, -1, H_DIFF, 2, DH_DIFF)
        v_past = cache_v[l, page_table].reshape(Bs, -1, H_DIFF, DV_DIFF)
        o_d = _diff_attn_sample(q_d, k_d, v_d, k_past, v_past, lam)
        xs = xs + gt_a * _mixer_out(o_r, g_r, o_d, g_ret_norm[l], g_diff_norm[l], lam_init, w_out[l])
        xs = xs + gt_f * _ffn(xs, sh_f, sc_f, g_norm_ffn[l], w_gate[l], w_up[l], w_down[l])
        k_s.append(k_d)
        v_s.append(v_d)
        S_s.append(S_new)

    return (x, xs, jnp.stack(k_p), jnp.stack(v_p), jnp.stack(S_p),
            jnp.stack(k_s), jnp.stack(v_s), jnp.stack(S_s))
```

```python
import functools
import math

import jax
import jax.numpy as jnp
from jax import lax
from jax.experimental import pallas as pl
from jax.experimental.pallas import tpu as pltpu

F32 = jnp.float32
BF16 = jnp.bfloat16

D_MODEL = 2048
N_HEADS = 4
D_HEAD = 256
D_MAP = 128
GROUP = 1024
N_GROUPS = 7
D_FF = 5632
N_MOD = 6
EPS = 1e-6
PAGE = 128
VMEM_LIMIT = 56 * 1024 * 1024

NT_DIMS = (((1,), (1,)), ((), ()))
TN_DIMS = (((0,), (0,)), ((), ()))


def _silu(x):
    return x / (1.0 + jnp.exp(-x))


def _rms(x, g):
    return x * lax.rsqrt(jnp.mean(x * x, axis=-1, keepdims=True) + EPS) * g


def _params(*sem):
    return pltpu.CompilerParams(dimension_semantics=sem, vmem_limit_bytes=VMEM_LIMIT)


def _adaln_kernel(c_ref, w_ref, b_ref, o_ref):
    a = _silu(c_ref[...]).astype(BF16)
    o_ref[0] = jnp.dot(a, w_ref[0].astype(BF16), preferred_element_type=F32) + b_ref[0]


def _adaln(c_all, w_ada, b_ada, tn=1024):
    depth, d, n = w_ada.shape
    rows = c_all.shape[0]
    return pl.pallas_call(
        _adaln_kernel,
        grid=(depth, n // tn),
        in_specs=[pl.BlockSpec((rows, d), lambda l, j: (0, 0)),
                  pl.BlockSpec((1, d, tn), lambda l, j: (l, 0, j)),
                  pl.BlockSpec((1, 1, tn), lambda l, j: (l, 0, j))],
        out_specs=pl.BlockSpec((1, rows, tn), lambda l, j: (l, 0, j)),
        out_shape=jax.ShapeDtypeStruct((depth, rows, n), F32),
        compiler_params=_params("parallel", "parallel"),
        name="adaln",
    )(c_all, w_ada, b_ada.reshape(depth, 1, n))


def _mixer_in_kernel(x_ref, sh_ref, sc_ref, gn_ref, w_ref, gq_ref, gk_ref,
                     obf_ref, g_ref, k_ref, v_ref, h_scr, *, q_scale):
    j = pl.program_id(1)

    @pl.when(j == 0)
    def _():
        y = _rms(x_ref[...], gn_ref[...])
        h_scr[...] = (y * (1.0 + sc_ref[...]) + sh_ref[...]).astype(BF16)

    acc = jnp.dot(h_scr[...], w_ref[...], preferred_element_type=F32)

    @pl.when(j == 0)
    def _():
        obf_ref[...] = acc.astype(BF16)

    @pl.when(j == 1)
    def _():
        obf_ref[...] = (acc * (D_HEAD ** -0.5)).astype(BF16)

    @pl.when(j == 2)
    def _():
        obf_ref[...] = acc.astype(BF16)

    @pl.when(j == 3)
    def _():
        g_ref[...] = acc

    @pl.when(j == 4)
    def _():
        for c in range(GROUP // D_MAP):
            sl = slice(c * D_MAP, (c + 1) * D_MAP)
            obf_ref[:, sl] = (_rms(acc[:, sl], gq_ref[...]) * q_scale).astype(BF16)

    @pl.when(j == 5)
    def _():
        for c in range(GROUP // D_MAP):
            sl = slice(c * D_MAP, (c + 1) * D_MAP)
            kn = _rms(acc[:, sl], gk_ref[...])
            k_ref[:, sl] = kn
            obf_ref[:, sl] = kn.astype(BF16)

    @pl.when(j == 6)
    def _():
        v_ref[...] = acc
        obf_ref[...] = acc.astype(BF16)


def _mixer_in(x, shift, scale, g_norm, w_bf, g_q, g_k, tm, q_scale):
    m, d = x.shape
    mod_rows = shift.shape[0]
    mod_blk = (1, d) if mod_rows == 1 else (tm, d)
    mod_map = (lambda i, j: (0, 0)) if mod_rows == 1 else (lambda i, j: (i, 0))
    f32_out = pl.BlockSpec((tm, GROUP), lambda i, j: (i, 0))
    return pl.pallas_call(
        functools.partial(_mixer_in_kernel, q_scale=q_scale),
        grid=(m // tm, N_GROUPS),
        in_specs=[pl.BlockSpec((tm, d), lambda i, j: (i, 0)),
                  pl.BlockSpec(mod_blk, mod_map),
                  pl.BlockSpec(mod_blk, mod_map),
                  pl.BlockSpec((1, d), lambda i, j: (0, 0)),
                  pl.BlockSpec((d, GROUP), lambda i, j: (0, j)),
                  pl.BlockSpec((1, D_MAP), lambda i, j: (0, 0)),
                  pl.BlockSpec((1, D_MAP), lambda i, j: (0, 0))],
        out_specs=[pl.BlockSpec((tm, GROUP), lambda i, j: (i, jnp.where(j >= 3, j - 1, j))),
                   f32_out, f32_out, f32_out],
        out_shape=[jax.ShapeDtypeStruct((m, 6 * GROUP), BF16),
                   jax.ShapeDtypeStruct((m, GROUP), F32),
                   jax.ShapeDtypeStruct((m, GROUP), F32),
                   jax.ShapeDtypeStruct((m, GROUP), F32)],
        scratch_shapes=[pltpu.VMEM((tm, d), BF16)],
        compiler_params=_params("parallel", "arbitrary"),
        name="mixer_in",
    )(x, shift, scale, g_norm.reshape(1, d), w_bf, g_q.reshape(1, D_MAP), g_k.reshape(1, D_MAP))


def _ret_prompt_kernel(lg_ref, q_ref, k_ref, v_ref, g_ref, gn_ref, r_ref, s_ref, dec_scr, *, chunk):
    h = pl.program_id(0)
    c = pl.program_id(1)
    lg = lg_ref[h]

    @pl.when(c == 0)
    def _():
        s_ref[...] = jnp.zeros_like(s_ref)
        i = lax.broadcasted_iota(jnp.int32, (chunk, chunk), 0)
        j = lax.broadcasted_iota(jnp.int32, (chunk, chunk), 1)
        diff = (i - j).astype(F32)
        dec_scr[...] = jnp.where(diff >= 0, jnp.exp(jnp.maximum(diff, 0.0) * lg), 0.0)

    q = q_ref[...]
    k = k_ref[...]
    v = v_ref[...]
    pos = lax.broadcasted_iota(jnp.int32, (chunk, 1), 0).astype(F32)
    q_decay = jnp.exp((pos + 1.0) * lg)
    k_decay = jnp.exp((chunk - 1.0 - pos) * lg)
    chunk_decay = jnp.exp(jnp.full((1, 1), chunk, F32) * lg)
    state = s_ref[0]

    scores = lax.dot_general(q, k, NT_DIMS, preferred_element_type=F32) * dec_scr[...]
    inner = jnp.dot(scores.astype(BF16), v, preferred_element_type=F32)
    cross = jnp.dot(q, state.astype(BF16), preferred_element_type=F32) * q_decay
    k_dec = (k.astype(F32) * k_decay).astype(BF16)
    s_ref[0] = chunk_decay * state + lax.dot_general(k_dec, v, TN_DIMS, preferred_element_type=F32)

    r_ref[...] = (_rms(inner + cross, gn_ref[0]) * _silu(g_ref[...])).astype(BF16)


def _ret_prompt(log_gamma, obf, g_r, g_ret_norm, chunk):
    t = obf.shape[0]
    return pl.pallas_call(
        functools.partial(_ret_prompt_kernel, chunk=chunk),
        grid=(N_HEADS, t // chunk),
        in_specs=[pl.BlockSpec(memory_space=pltpu.SMEM),
                  pl.BlockSpec((chunk, D_HEAD), lambda h, c: (c, h)),
                  pl.BlockSpec((chunk, D_HEAD), lambda h, c: (c, N_HEADS + h)),
                  pl.BlockSpec((chunk, D_HEAD), lambda h, c: (c, 2 * N_HEADS + h)),
                  pl.BlockSpec((chunk, D_HEAD), lambda h, c: (c, h)),
                  pl.BlockSpec((1, 1, D_HEAD), lambda h, c: (h, 0, 0))],
        out_specs=[pl.BlockSpec((chunk, D_HEAD), lambda h, c: (c, h)),
                   pl.BlockSpec((1, D_HEAD, D_HEAD), lambda h, c: (h, 0, 0))],
        out_shape=[jax.ShapeDtypeStruct((t, GROUP), BF16),
                   jax.ShapeDtypeStruct((N_HEADS, D_HEAD, D_HEAD), F32)],
        scratch_shapes=[pltpu.VMEM((chunk, chunk), F32)],
        compiler_params=_params("parallel", "arbitrary"),
        name="ret_prompt",
    )(log_gamma, obf, obf, obf, g_r, g_ret_norm.reshape(N_HEADS, 1, D_HEAD))


def _lambda(lq1_ref, lk1_ref, lq2_ref, lk2_ref, lam_init):
    s1 = jnp.sum(lq1_ref[...] * lk1_ref[...], axis=-1, keepdims=True)
    s2 = jnp.sum(lq2_ref[...] * lk2_ref[...], axis=-1, keepdims=True)
    return jnp.exp(s1) - jnp.exp(s2) + lam_init


def _diff_prompt_kernel(qi_tab, ki_tab, slope_ref, q_ref, k_ref, v_ref,
                        lq1_ref, lk1_ref, lq2_ref, lk2_ref, gd_ref, o_ref,
                        m_scr, l_scr, acc_scr, *, tile, lam_init):
    h = pl.program_id(0)
    t = pl.program_id(1)
    qi = qi_tab[t]
    ki = ki_tab[t]
    slope = slope_ref[h]

    @pl.when(ki == 0)
    def _():
        m_scr[...] = jnp.full_like(m_scr, -jnp.inf)
        l_scr[...] = jnp.zeros_like(l_scr)
        acc_scr[...] = jnp.zeros_like(acc_scr)

    def step(masked):
        q = q_ref[...]
        k = k_ref[...]
        v = v_ref[...]
        col = lax.broadcasted_iota(jnp.int32, (1, tile), 1)
        bias = slope * (col + (ki - qi) * tile).astype(F32)
        if masked:
            row_i = lax.broadcasted_iota(jnp.int32, (tile, tile), 0)
            col_i = lax.broadcasted_iota(jnp.int32, (tile, tile), 1)
            keep = col_i <= row_i
        for m in range(2):
            sl = slice(m * D_MAP, (m + 1) * D_MAP)
            s = lax.dot_general(q[:, sl], k[:, sl], NT_DIMS, preferred_element_type=F32) + bias
            if masked:
                s = jnp.where(keep, s, -jnp.inf)
            m_old = m_scr[m][:, :1]
            m_new = jnp.maximum(m_old, jnp.max(s, axis=-1, keepdims=True))
            alpha = jnp.exp(m_old - m_new)
            p = jnp.exp(s - m_new)
            l_new = alpha * l_scr[m][:, :1] + jnp.sum(p, axis=-1, keepdims=True)
            acc_scr[m] = alpha * acc_scr[m] + jnp.dot(p.astype(BF16), v, preferred_element_type=F32)
            m_scr[m] = jnp.broadcast_to(m_new, (tile, 128))
            l_scr[m] = jnp.broadcast_to(l_new, (tile, 128))

    @pl.when(ki < qi)
    def _():
        step(False)

    @pl.when(ki == qi)
    def _():
        step(True)
        lam = _lambda(lq1_ref, lk1_ref, lq2_ref, lk2_ref, lam_init)
        o = acc_scr[0] / l_scr[0][:, :1] - lam * (acc_scr[1] / l_scr[1][:, :1])
        o_ref[...] = (_rms(o, gd_ref[...]) * (1.0 - lam_init)).astype(BF16)


def _diff_prompt(slopes, obf, lam_vecs, g_diff_norm, lam_init, tile):
    t = obf.shape[0]
    nq = t // tile
    pairs = [(qi, ki) for qi in range(nq) for ki in range(qi + 1)]
    qi_tab = jnp.asarray([p[0] for p in pairs], jnp.int32)
    ki_tab = jnp.asarray([p[1] for p in pairs], jnp.int32)
    q_blk, k_blk, v_blk = 3 * N_HEADS, 4 * N_HEADS, 5 * N_HEADS
    vec = pl.BlockSpec((1, D_MAP), lambda h, t, qt, kt: (0, 0))
    grid_spec = pltpu.PrefetchScalarGridSpec(
        num_scalar_prefetch=2,
        grid=(N_HEADS, len(pairs)),
        in_specs=[pl.BlockSpec(memory_space=pltpu.SMEM),
                  pl.BlockSpec((tile, D_HEAD), lambda h, t, qt, kt: (qt[t], q_blk + h)),
                  pl.BlockSpec((tile, D_HEAD), lambda h, t, qt, kt: (kt[t], k_blk + h)),
                  pl.BlockSpec((tile, D_HEAD), lambda h, t, qt, kt: (kt[t], v_blk + h)),
                  vec, vec, vec, vec,
                  pl.BlockSpec((1, D_HEAD), lambda h, t, qt, kt: (0, 0))],
        out_specs=pl.BlockSpec((tile, D_HEAD), lambda h, t, qt, kt: (qt[t], h)),
        scratch_shapes=[pltpu.VMEM((2, tile, 128), F32),
                        pltpu.VMEM((2, tile, 128), F32),
                        pltpu.VMEM((2, tile, D_HEAD), F32)],
    )
    return pl.pallas_call(
        functools.partial(_diff_prompt_kernel, tile=tile, lam_init=lam_init),
        grid_spec=grid_spec,
        out_shape=jax.ShapeDtypeStruct((t, GROUP), BF16),
        compiler_params=_params("parallel", "arbitrary"),
        name="diff_prompt",
    )(qi_tab, ki_tab, slopes, obf, obf, obf, *lam_vecs, g_diff_norm.reshape(1, D_HEAD))


def _mixer_out_kernel(r_ref, d_ref, wr_ref, wd_ref, x_ref, gt_ref, sh_ref, sc_ref, gn_ref, xo_ref, h_ref):
    o = (jnp.dot(r_ref[...], wr_ref[...], preferred_element_type=F32)
         + jnp.dot(d_ref[...], wd_ref[...], preferred_element_type=F32))
    x = x_ref[...] + gt_ref[...] * o
    xo_ref[...] = x
    h_ref[...] = (_rms(x, gn_ref[...]) * (1.0 + sc_ref[...]) + sh_ref[...]).astype(BF16)


def _mixer_out(r, dd, w_out_bf, x, gate, shift, scale, g_norm, tm):
    m, d = x.shape
    mod_rows = gate.shape[0]
    mod = pl.BlockSpec((1, d), lambda i: (0, 0)) if mod_rows == 1 else pl.BlockSpec((tm, d), lambda i: (i, 0))
    return pl.pallas_call(
        _mixer_out_kernel,
        grid=(m // tm,),
        in_specs=[pl.BlockSpec((tm, GROUP), lambda i: (i, 0)),
                  pl.BlockSpec((tm, GROUP), lambda i: (i, 0)),
                  pl.BlockSpec((GROUP, d), lambda i: (0, 0)),
                  pl.BlockSpec((GROUP, d), lambda i: (1, 0)),
                  pl.BlockSpec((tm, d), lambda i: (i, 0)),
                  mod, mod, mod,
                  pl.BlockSpec((1, d), lambda i: (0, 0))],
        out_specs=[pl.BlockSpec((tm, d), lambda i: (i, 0)),
                   pl.BlockSpec((tm, d), lambda i: (i, 0))],
        out_shape=[jax.ShapeDtypeStruct((m, d), F32), jax.ShapeDtypeStruct((m, d), BF16)],
        compiler_params=_params("parallel"),
        name="mixer_out",
    )(r, dd, w_out_bf, w_out_bf, x, gate, shift, scale, g_norm.reshape(1, d))


def _ffn_kernel(h_ref, wg_ref, wu_ref, wd_ref, x_ref, gt_ref, o_ref, acc_scr):
    f = pl.program_id(1)
    h = h_ref[...]
    a = (_silu(jnp.dot(h, wg_ref[...], preferred_element_type=F32))
         * jnp.dot(h, wu_ref[...], preferred_element_type=F32)).astype(BF16)
    part = jnp.dot(a, wd_ref[...], preferred_element_type=F32)

    @pl.when(f == 0)
    def _():
        acc_scr[...] = part

    @pl.when(f > 0)
    def _():
        acc_scr[...] += part

    @pl.when(f == pl.num_programs(1) - 1)
    def _():
        o_ref[...] = x_ref[...] + gt_ref[...] * acc_scr[...]


def _ffn(h, wg_bf, wu_bf, wd_bf, x, gate, tm, tf):
    m, d = x.shape
    ff = wg_bf.shape[1]
    mod_rows = gate.shape[0]
    mod = (pl.BlockSpec((1, d), lambda i, f: (0, 0)) if mod_rows == 1
           else pl.BlockSpec((tm, d), lambda i, f: (i, 0)))
    return pl.pallas_call(
        _ffn_kernel,
        grid=(m // tm, ff // tf),
        in_specs=[pl.BlockSpec((tm, d), lambda i, f: (i, 0)),
                  pl.BlockSpec((d, tf), lambda i, f: (0, f)),
                  pl.BlockSpec((d, tf), lambda i, f: (0, f)),
                  pl.BlockSpec((tf, d), lambda i, f: (f, 0)),
                  pl.BlockSpec((tm, d), lambda i, f: (i, 0)),
                  mod],
        out_specs=pl.BlockSpec((tm, d), lambda i, f: (i, 0)),
        out_shape=jax.ShapeDtypeStruct((m, d), F32),
        scratch_shapes=[pltpu.VMEM((tm, d), F32)],
        compiler_params=_params("parallel", "arbitrary"),
        name="ffn",
    )(h, wg_bf, wu_bf, wd_bf, x, gate)


def _ret_sample_kernel(lg_ref, s_ref, q_ref, k_ref, v_ref, g_ref, gn_ref, so_ref, r_ref):
    for h in range(N_HEADS):
        gamma = jnp.exp(jnp.full((1, 1), lg_ref[h], F32))
        state = s_ref[0, h]
        qc = q_ref[0, h]
        kc = k_ref[0, h]
        vr = v_ref[0, h]
        so_ref[0, h] = gamma * state + kc * vr
        qk = jnp.sum(qc * kc, axis=0, keepdims=True)
        o = qk * vr + gamma * jnp.sum(qc * state, axis=0, keepdims=True)
        r_ref[0, h] = _rms(o, gn_ref[h]) * _silu(g_ref[0, h])


def _ret_sample(log_gamma, state, q_col, k_col, v_row, g_row, g_ret_norm):
    b = state.shape[0]
    col = pl.BlockSpec((1, N_HEADS, D_HEAD, 1), lambda i: (i, 0, 0, 0))
    row = pl.BlockSpec((1, N_HEADS, 1, D_HEAD), lambda i: (i, 0, 0, 0))
    st = pl.BlockSpec((1, N_HEADS, D_HEAD, D_HEAD), lambda i: (i, 0, 0, 0))
    return pl.pallas_call(
        _ret_sample_kernel,
        grid=(b,),
        in_specs=[pl.BlockSpec(memory_space=pltpu.SMEM), st, col, col, row, row,
                  pl.BlockSpec((N_HEADS, 1, D_HEAD), lambda i: (0, 0, 0))],
        out_specs=[st, row],
        out_shape=[jax.ShapeDtypeStruct(state.shape, F32),
                   jax.ShapeDtypeStruct((b, N_HEADS, 1, D_HEAD), F32)],
        compiler_params=_params("parallel"),
        name="ret_sample",
    )(log_gamma, state, q_col, k_col, v_row, g_row, g_ret_norm.reshape(N_HEADS, 1, D_HEAD))


def _diff_sample_kernel(pt_ref, q_ref, kn_ref, vn_ref, slope_ref, lq1_ref, lk1_ref, lq2_ref, lk2_ref,
                        gd_ref, *rest, n_pages, past_len, lam_init):
    k_refs = rest[:n_pages]
    v_refs = rest[n_pages:2 * n_pages]
    o_ref, wq_scr, logit_scr, lnew_scr, p_scr, pnew_scr, acc_scr = rest[2 * n_pages:]
    s = pl.program_id(1)
    n_ksteps = past_len // (n_pages * PAGE)
    n_maps = 2 * N_HEADS

    @pl.when(s == 0)
    def _():
        q_pad = jnp.concatenate([q_ref[0], jnp.zeros((D_MAP - n_maps, D_MAP), F32)], axis=0)
        q_t = q_pad.T
        lane = lax.broadcasted_iota(jnp.int32, (D_MAP, 128), 1)
        for c in range(n_maps):
            wq_scr[c * D_MAP:(c + 1) * D_MAP, :] = jnp.where(lane == c, q_t, 0.0).astype(BF16)
        k_new = jnp.broadcast_to(kn_ref[0], (8, GROUP)).astype(BF16)
        lnew_scr[...] = jnp.dot(k_new, wq_scr[...], preferred_element_type=F32)

    @pl.when(s < n_ksteps)
    def _():
        row = lax.broadcasted_iota(jnp.int32, (PAGE, 128), 0)
        for i in range(n_pages):
            tok0 = pl.multiple_of((s * n_pages + i) * PAGE, PAGE)
            logits = jnp.dot(k_refs[i][...].astype(BF16), wq_scr[...], preferred_element_type=F32)
            dist = (past_len - tok0 - row).astype(F32)
            logit_scr[pl.ds(tok0, PAGE), :] = logits - slope_ref[...] * dist

    @pl.when(s == n_ksteps)
    def _():
        logits = logit_scr[...]
        l_new = lnew_scr[0:1, :]
        m = jnp.maximum(jnp.max(logits, axis=0, keepdims=True), l_new)
        e = jnp.exp(logits - m)
        e_new = jnp.exp(l_new - m)
        inv = 1.0 / (jnp.sum(e, axis=0, keepdims=True) + e_new)
        lam = _lambda(lq1_ref, lk1_ref, lq2_ref, lk2_ref, lam_init)
        lane = lax.broadcasted_iota(jnp.int32, (1, 128), 1)
        coef = jnp.where(lane % 2 == 1, lam, 1.0) * inv
        p_scr[...] = (e * coef).astype(BF16)
        pnew_scr[...] = jnp.broadcast_to(e_new * coef, (8, 128))
        acc_scr[...] = jnp.zeros_like(acc_scr)

    def spread():
        c = lax.broadcasted_iota(jnp.int32, (128, GROUP), 0)
        head = lax.broadcasted_iota(jnp.int32, (128, GROUP), 1) // D_HEAD
        return jnp.where(c == 2 * head, 1.0, jnp.where(c == 2 * head + 1, -1.0, 0.0)).astype(BF16)

    @pl.when(s >= n_ksteps)
    def _():
        rmat = spread()
        acc = acc_scr[...]
        for i in range(n_pages):
            tok0 = pl.multiple_of(((s - n_ksteps) * n_pages + i) * PAGE, PAGE)
            w = jnp.dot(p_scr[pl.ds(tok0, PAGE), :], rmat, preferred_element_type=F32)
            acc = acc + jnp.sum((w * v_refs[i][...]).reshape(PAGE // 8, 8, GROUP), axis=0)
        acc_scr[...] = acc

    @pl.when(s == pl.num_programs(1) - 1)
    def _():
        w_new = jnp.dot(pnew_scr[...].astype(BF16), spread(), preferred_element_type=F32)[0:1, :]
        out = jnp.sum(acc_scr[...], axis=0, keepdims=True) + w_new * vn_ref[0]
        for h in range(N_HEADS):
            sl = slice(h * D_HEAD, (h + 1) * D_HEAD)
            o_ref[0, :, sl] = _rms(out[:, sl], gd_ref[...]) * (1.0 - lam_init)


def _diff_sample(page_table, q_maps, k_new, v_new, cache_k, cache_v, layer, slope_lanes, lam_vecs,
                 g_diff_norm, lam_init, n_pages):
    b, pages_per_seq = page_table.shape
    past_len = pages_per_seq * PAGE
    n_ksteps = pages_per_seq // n_pages

    def k_spec(i):
        return pl.BlockSpec((None, None, PAGE, GROUP),
                            lambda bi, s, pt: (layer, pt[bi, jnp.minimum(s, n_ksteps - 1) * n_pages + i], 0, 0))

    def v_spec(i):
        return pl.BlockSpec((None, None, PAGE, GROUP),
                            lambda bi, s, pt: (layer, pt[bi, jnp.maximum(s - n_ksteps, 0) * n_pages + i], 0, 0))

    vec = pl.BlockSpec((1, D_MAP), lambda bi, s, pt: (0, 0))
    tok = pl.BlockSpec((1, 1, GROUP), lambda bi, s, pt: (bi, 0, 0))
    grid_spec = pltpu.PrefetchScalarGridSpec(
        num_scalar_prefetch=1,
        grid=(b, 2 * n_ksteps),
        in_specs=[pl.BlockSpec((1, 2 * N_HEADS, D_MAP), lambda bi, s, pt: (bi, 0, 0)),
                  tok, tok, vec, vec, vec, vec, vec,
                  pl.BlockSpec((1, D_HEAD), lambda bi, s, pt: (0, 0))]
                 + [k_spec(i) for i in range(n_pages)] + [v_spec(i) for i in range(n_pages)],
        out_specs=tok,
        scratch_shapes=[pltpu.VMEM((GROUP, 128), BF16),
                        pltpu.VMEM((past_len, 128), F32),
                        pltpu.VMEM((8, 128), F32),
                        pltpu.VMEM((past_len, 128), BF16),
                        pltpu.VMEM((8, 128), F32),
                        pltpu.VMEM((8, GROUP), F32)],
    )
    return pl.pallas_call(
        functools.partial(_diff_sample_kernel, n_pages=n_pages, past_len=past_len, lam_init=lam_init),
        grid_spec=grid_spec,
        out_shape=jax.ShapeDtypeStruct((b, 1, GROUP), F32),
        compiler_params=_params("parallel", "arbitrary"),
        name="diff_sample",
    )(page_table, q_maps, k_new, v_new, slope_lanes, *lam_vecs, g_diff_norm.reshape(1, D_HEAD),
      *([cache_k] * n_pages), *([cache_v] * n_pages))


def _dense_tail(r, dd, x, mods, w_out_bf, g_norm_ffn, wg_bf, wu_bf, wd_bf, tm, tf):
    _, _, gt_a, sh_f, sc_f, gt_f = mods
    x, h = _mixer_out(r, dd, w_out_bf, x, gt_a, sh_f, sc_f, g_norm_ffn, tm)
    return _ffn(h, wg_bf, wu_bf, wd_bf, x, gt_f, tm, tf)


def kernel(x_prompt, x_sample, cache_k, cache_v, state_ret, page_table, c_prompt, c_sample,
           w_ada, b_ada, g_norm_mix, w_in, g_qnorm, g_knorm, lambda_q1, lambda_k1, lambda_q2,
           lambda_k2, g_ret_norm, g_diff_norm, w_out, g_norm_ffn, w_gate, w_up, w_down):
    depth = w_in.shape[0]
    n_prompt, t, d = x_prompt.shape
    assert n_prompt == 1
    bs = x_sample.shape[0]
    n_pool = cache_k.shape[1]

    log_gamma = jnp.log1p(-jnp.exp2(-5.0 - jnp.arange(N_HEADS, dtype=F32)))
    slopes = jnp.exp2(-8.0 * (jnp.arange(N_HEADS, dtype=F32) + 1.0) / N_HEADS)
    slope_lanes = jnp.zeros((1, 128), F32).at[0, :2 * N_HEADS].set(jnp.repeat(slopes, 2))
    q_scale = D_MAP ** -0.5

    pad = (-(bs + 1)) % 8
    c_all = jnp.concatenate([c_sample, c_prompt, jnp.zeros((pad, d), F32)], axis=0)
    mod = _adaln(c_all, w_ada, b_ada)

    w_in_bf, w_out_bf = w_in.astype(BF16), w_out.astype(BF16)
    wg_bf, wu_bf, wd_bf = w_gate.astype(BF16), w_up.astype(BF16), w_down.astype(BF16)
    cache_k2 = cache_k.reshape(depth, n_pool, PAGE, GROUP)
    cache_v2 = cache_v.reshape(depth, n_pool, PAGE, GROUP)

    x = x_prompt.reshape(t, d)
    xs = x_sample.reshape(bs, d)
    k_p, v_p, s_p, k_s, v_s, s_s = [], [], [], [], [], []
    for l in range(depth):
        lam_init = 0.8 - 0.6 * math.exp(-0.3 * l)
        lam_vecs = [a[l].reshape(1, D_MAP) for a in (lambda_q1, lambda_k1, lambda_q2, lambda_k2)]
        mods_p = [mod[l, bs:bs + 1, i * d:(i + 1) * d] for i in range(N_MOD)]
        mods_s = [mod[l, :bs, i * d:(i + 1) * d] for i in range(N_MOD)]

        obf, g_r, k_d, v_d = _mixer_in(x, mods_p[0], mods_p[1], g_norm_mix[l], w_in_bf[l],
                                       g_qnorm[l], g_knorm[l], 512, q_scale)
        r, s_fin = _ret_prompt(log_gamma, obf, g_r, g_ret_norm[l], 256)
        dd = _diff_prompt(slopes, obf, lam_vecs, g_diff_norm[l], lam_init, 512)
        x = _dense_tail(r, dd, x, mods_p, w_out_bf[l], g_norm_ffn[l], wg_bf[l], wu_bf[l], wd_bf[l], 512, 512)
        k_p.append(k_d.reshape(1, t, N_HEADS, 2, D_MAP))
        v_p.append(v_d.reshape(1, t, N_HEADS, D_HEAD))
        s_p.append(s_fin.reshape(1, N_HEADS, D_HEAD, D_HEAD))

        obf, g_r, k_d, v_d = _mixer_in(xs, mods_s[0], mods_s[1], g_norm_mix[l], w_in_bf[l],
                                       g_qnorm[l], g_knorm[l], bs, q_scale)
        q_col = obf[:, :GROUP].astype(F32).reshape(bs, N_HEADS, D_HEAD, 1)
        k_col = obf[:, GROUP:2 * GROUP].astype(F32).reshape(bs, N_HEADS, D_HEAD, 1)
        v_row = obf[:, 2 * GROUP:3 * GROUP].astype(F32).reshape(bs, N_HEADS, 1, D_HEAD)
        s_new, r = _ret_sample(log_gamma, state_ret[l], q_col, k_col, v_row,
                               g_r.reshape(bs, N_HEADS, 1, D_HEAD), g_ret_norm[l])
        q_maps = obf[:, 3 * GROUP:4 * GROUP].astype(F32).reshape(bs, 2 * N_HEADS, D_MAP)
        dd = _diff_sample(page_table, q_maps, k_d.reshape(bs, 1, GROUP), v_d.reshape(bs, 1, GROUP),
                          cache_k2, cache_v2, l, slope_lanes, lam_vecs, g_diff_norm[l], lam_init, 8)
        xs = _dense_tail(r.reshape(bs, GROUP).astype(BF16), dd.reshape(bs, GROUP).astype(BF16), xs, mods_s,
                         w_out_bf[l], g_norm_ffn[l], wg_bf[l], wu_bf[l], wd_bf[l], bs, 512)
        k_s.append(k_d.reshape(bs, 1, N_HEADS, 2, D_MAP))
        v_s.append(v_d.reshape(bs, 1, N_HEADS, D_HEAD))
        s_s.append(s_new)

    return (x.reshape(1, t, d), xs.reshape(bs, 1, d), jnp.stack(k_p), jnp.stack(v_p), jnp.stack(s_p),
            jnp.stack(k_s), jnp.stack(v_s), jnp.stack(s_s))
```

```python
import functools
import math

import jax
import jax.numpy as jnp
from jax import lax
from jax.experimental import pallas as pl
from jax.experimental.pallas import tpu as pltpu

F32 = jnp.float32
BF16 = jnp.bfloat16

D_MODEL = 2048
N_HEADS = 4
D_HEAD = 256
D_MAP = 128
GROUP = 1024
N_GROUPS = 7
D_FF = 5632
N_MOD = 6
EPS = 1e-6
PAGE = 128
VMEM_LIMIT = 56 * 1024 * 1024
LOG2E = math.log2(math.e)

NT_DIMS = (((1,), (1,)), ((), ()))
TN_DIMS = (((0,), (0,)), ((), ()))


def _silu(x):
    return x / (1.0 + jnp.exp(-x))


def _rms(x, g):
    return x * lax.rsqrt(jnp.mean(x * x, axis=-1, keepdims=True) + EPS) * g


def _params(*sem):
    return pltpu.CompilerParams(dimension_semantics=sem, vmem_limit_bytes=VMEM_LIMIT)


def _adaln_kernel(c_ref, w_ref, b_ref, o_ref):
    a = _silu(c_ref[...]).astype(BF16)
    o_ref[0] = jnp.dot(a, w_ref[0].astype(BF16), preferred_element_type=F32) + b_ref[0]


def _adaln(c_all, w_ada, b_ada, tn=1024):
    depth, d, n = w_ada.shape
    rows = c_all.shape[0]
    return pl.pallas_call(
        _adaln_kernel,
        grid=(depth, n // tn),
        in_specs=[pl.BlockSpec((rows, d), lambda l, j: (0, 0)),
                  pl.BlockSpec((1, d, tn), lambda l, j: (l, 0, j)),
                  pl.BlockSpec((1, 1, tn), lambda l, j: (l, 0, j))],
        out_specs=pl.BlockSpec((1, rows, tn), lambda l, j: (l, 0, j)),
        out_shape=jax.ShapeDtypeStruct((depth, rows, n), F32),
        compiler_params=_params("parallel", "parallel"),
        name="adaln",
    )(c_all, w_ada, b_ada.reshape(depth, 1, n))


def _mixer_in_kernel(x_ref, sh_ref, sc_ref, gn_ref, w_ref, gq_ref, gk_ref,
                     obf_ref, g_ref, k_ref, v_ref, *rest, q_scale, with_vt):
    h_scr = rest[-1]
    j = pl.program_id(1)

    @pl.when(j == 0)
    def _():
        y = _rms(x_ref[...], gn_ref[...])
        h_scr[...] = (y * (1.0 + sc_ref[...]) + sh_ref[...]).astype(BF16)

    acc = jnp.dot(h_scr[...], w_ref[...], preferred_element_type=F32)

    @pl.when(j == 0)
    def _():
        obf_ref[...] = acc.astype(BF16)

    @pl.when(j == 1)
    def _():
        obf_ref[...] = (acc * (D_HEAD ** -0.5)).astype(BF16)

    @pl.when(j == 2)
    def _():
        obf_ref[...] = acc.astype(BF16)

    @pl.when(j == 3)
    def _():
        g_ref[...] = acc

    @pl.when(j == 4)
    def _():
        for c in range(GROUP // D_MAP):
            sl = slice(c * D_MAP, (c + 1) * D_MAP)
            obf_ref[:, sl] = (_rms(acc[:, sl], gq_ref[...]) * q_scale).astype(BF16)

    @pl.when(j == 5)
    def _():
        for c in range(GROUP // D_MAP):
            sl = slice(c * D_MAP, (c + 1) * D_MAP)
            kn = _rms(acc[:, sl], gk_ref[...])
            k_ref[:, sl] = kn
            obf_ref[:, sl] = kn.astype(BF16)

    @pl.when(j == 6)
    def _():
        v_ref[...] = acc
        obf_ref[...] = acc.astype(BF16)
        if with_vt:
            rest[0][...] = acc.T.astype(BF16)


def _mixer_in(x, shift, scale, g_norm, w_bf, g_q, g_k, tm, q_scale, with_vt=False):
    m, d = x.shape
    mod_rows = shift.shape[0]
    mod_blk = (1, d) if mod_rows == 1 else (tm, d)
    mod_map = (lambda i, j: (0, 0)) if mod_rows == 1 else (lambda i, j: (i, 0))
    f32_out = pl.BlockSpec((tm, GROUP), lambda i, j: (i, 0))
    vt_spec = [pl.BlockSpec((GROUP, tm), lambda i, j: (0, i))] if with_vt else []
    vt_shape = [jax.ShapeDtypeStruct((GROUP, m), BF16)] if with_vt else []
    return pl.pallas_call(
        functools.partial(_mixer_in_kernel, q_scale=q_scale, with_vt=with_vt),
        grid=(m // tm, N_GROUPS),
        in_specs=[pl.BlockSpec((tm, d), lambda i, j: (i, 0)),
                  pl.BlockSpec(mod_blk, mod_map),
                  pl.BlockSpec(mod_blk, mod_map),
                  pl.BlockSpec((1, d), lambda i, j: (0, 0)),
                  pl.BlockSpec((d, GROUP), lambda i, j: (0, j)),
                  pl.BlockSpec((1, D_MAP), lambda i, j: (0, 0)),
                  pl.BlockSpec((1, D_MAP), lambda i, j: (0, 0))],
        out_specs=[pl.BlockSpec((tm, GROUP), lambda i, j: (i, jnp.where(j >= 3, j - 1, j))),
                   f32_out, f32_out, f32_out] + vt_spec,
        out_shape=[jax.ShapeDtypeStruct((m, 6 * GROUP), BF16),
                   jax.ShapeDtypeStruct((m, GROUP), F32),
                   jax.ShapeDtypeStruct((m, GROUP), F32),
                   jax.ShapeDtypeStruct((m, GROUP), F32)] + vt_shape,
        scratch_shapes=[pltpu.VMEM((tm, d), BF16)],
        compiler_params=_params("parallel", "arbitrary"),
        name="mixer_in",
    )(x, shift, scale, g_norm.reshape(1, d), w_bf, g_q.reshape(1, D_MAP), g_k.reshape(1, D_MAP))


def _ret_prompt_kernel(lg_ref, q_ref, k_ref, v_ref, g_ref, gn_ref, r_ref, s_ref, dec_scr, *, chunk):
    h = pl.program_id(0)
    c = pl.program_id(1)
    lg = lg_ref[h]

    @pl.when(c == 0)
    def _():
        s_ref[...] = jnp.zeros_like(s_ref)
        i = lax.broadcasted_iota(jnp.int32, (chunk, chunk), 0)
        j = lax.broadcasted_iota(jnp.int32, (chunk, chunk), 1)
        diff = (i - j).astype(F32)
        dec_scr[...] = jnp.where(diff >= 0, jnp.exp(jnp.maximum(diff, 0.0) * lg), 0.0)

    q = q_ref[...]
    k = k_ref[...]
    v = v_ref[...]
    pos = lax.broadcasted_iota(jnp.int32, (chunk, 1), 0).astype(F32)
    q_decay = jnp.exp((pos + 1.0) * lg)
    k_decay = jnp.exp((chunk - 1.0 - pos) * lg)
    chunk_decay = jnp.exp(jnp.full((1, 1), chunk, F32) * lg)
    state = s_ref[0]

    scores = lax.dot_general(q, k, NT_DIMS, preferred_element_type=F32) * dec_scr[...]
    inner = jnp.dot(scores.astype(BF16), v, preferred_element_type=F32)
    cross = jnp.dot(q, state.astype(BF16), preferred_element_type=F32) * q_decay
    k_dec = (k.astype(F32) * k_decay).astype(BF16)
    s_ref[0] = chunk_decay * state + lax.dot_general(k_dec, v, TN_DIMS, preferred_element_type=F32)

    r_ref[...] = (_rms(inner + cross, gn_ref[0]) * _silu(g_ref[...])).astype(BF16)


def _ret_prompt(log_gamma, obf, g_r, g_ret_norm, chunk):
    t = obf.shape[0]
    return pl.pallas_call(
        functools.partial(_ret_prompt_kernel, chunk=chunk),
        grid=(N_HEADS, t // chunk),
        in_specs=[pl.BlockSpec(memory_space=pltpu.SMEM),
                  pl.BlockSpec((chunk, D_HEAD), lambda h, c: (c, h)),
                  pl.BlockSpec((chunk, D_HEAD), lambda h, c: (c, N_HEADS + h)),
                  pl.BlockSpec((chunk, D_HEAD), lambda h, c: (c, 2 * N_HEADS + h)),
                  pl.BlockSpec((chunk, D_HEAD), lambda h, c: (c, h)),
                  pl.BlockSpec((1, 1, D_HEAD), lambda h, c: (h, 0, 0))],
        out_specs=[pl.BlockSpec((chunk, D_HEAD), lambda h, c: (c, h)),
                   pl.BlockSpec((1, D_HEAD, D_HEAD), lambda h, c: (h, 0, 0))],
        out_shape=[jax.ShapeDtypeStruct((t, GROUP), BF16),
                   jax.ShapeDtypeStruct((N_HEADS, D_HEAD, D_HEAD), F32)],
        scratch_shapes=[pltpu.VMEM((chunk, chunk), F32)],
        compiler_params=_params("parallel", "arbitrary"),
        name="ret_prompt",
    )(log_gamma, obf, obf, obf, g_r, g_ret_norm.reshape(N_HEADS, 1, D_HEAD))


def _lambda(lq1_ref, lk1_ref, lq2_ref, lk2_ref, lam_init):
    s1 = jnp.sum(lq1_ref[...] * lk1_ref[...], axis=-1, keepdims=True)
    s2 = jnp.sum(lq2_ref[...] * lk2_ref[...], axis=-1, keepdims=True)
    return jnp.exp(s1) - jnp.exp(s2) + lam_init


def _diff_prompt_kernel(qi_tab, ki_tab, slope_ref, q_ref, k_ref, vt_ref,
                        lq1_ref, lk1_ref, lq2_ref, lk2_ref, gd_ref, o_ref,
                        m_scr, l_scr, acc_scr, *, tq, tk, lam_init):
    h = pl.program_id(0)
    t = pl.program_id(1)
    qi = qi_tab[t]
    ki = ki_tab[t]
    ratio = tq // tk
    slope = slope_ref[h] * LOG2E

    @pl.when(ki == 0)
    def _():
        m_scr[...] = jnp.full_like(m_scr, -jnp.inf)
        l_scr[...] = jnp.zeros_like(l_scr)
        acc_scr[...] = jnp.zeros_like(acc_scr)

    def step(masked):
        q = q_ref[...]
        k = k_ref[...]
        vt = vt_ref[...]
        krow = lax.broadcasted_iota(jnp.int32, (tk, 128), 0)
        bias = pltpu.repeat(slope * (krow + (ki * tk - qi * tq)).astype(F32), tq // 128, axis=1)
        if masked:
            kpos = lax.broadcasted_iota(jnp.int32, (tk, tq), 0) + (ki * tk - qi * tq)
            keep = kpos <= lax.broadcasted_iota(jnp.int32, (tk, tq), 1)
        for m in range(2):
            sl = slice(m * D_MAP, (m + 1) * D_MAP)
            s = lax.dot_general(k[:, sl], q[:, sl], NT_DIMS, preferred_element_type=F32) + bias
            if masked:
                s = jnp.where(keep, s, -jnp.inf)
            m_old = m_scr[m]
            m_new = jnp.maximum(m_old, jnp.max(s, axis=0, keepdims=True))
            alpha = jnp.exp2(m_old - m_new)
            p = jnp.exp2(s - m_new)
            l_scr[m] = alpha * l_scr[m] + jnp.sum(p, axis=0, keepdims=True)
            acc_scr[m] = alpha * acc_scr[m] + jnp.dot(vt, p.astype(BF16), preferred_element_type=F32)
            m_scr[m] = m_new

    @pl.when(ki < qi * ratio)
    def _():
        step(False)

    @pl.when(ki >= qi * ratio)
    def _():
        step(True)

    @pl.when(ki == (qi + 1) * ratio - 1)
    def _():
        lam = _lambda(lq1_ref, lk1_ref, lq2_ref, lk2_ref, lam_init)
        o = acc_scr[0] * (1.0 / l_scr[0]) - lam * (acc_scr[1] * (1.0 / l_scr[1]))
        y = o * lax.rsqrt(jnp.mean(o * o, axis=0, keepdims=True) + EPS)
        y = y * pltpu.repeat(gd_ref[...], tq // 128, axis=1) * (1.0 - lam_init)
        o_ref[...] = y.T.astype(BF16)


def _diff_prompt(slopes, obf, vt, lam_vecs, g_diff_norm, lam_init, tq, tk):
    t = obf.shape[0]
    ratio = tq // tk
    pairs = [(qi, ki) for qi in range(t // tq) for ki in range((qi + 1) * ratio)]
    qi_tab = jnp.asarray([p[0] for p in pairs], jnp.int32)
    ki_tab = jnp.asarray([p[1] for p in pairs], jnp.int32)
    q_blk, k_blk = 3 * N_HEADS, 4 * N_HEADS
    vec = pl.BlockSpec((1, D_MAP), lambda h, t, qt, kt: (0, 0))
    gd_rep = jnp.broadcast_to(g_diff_norm.reshape(D_HEAD, 1), (D_HEAD, 128))
    grid_spec = pltpu.PrefetchScalarGridSpec(
        num_scalar_prefetch=2,
        grid=(N_HEADS, len(pairs)),
        in_specs=[pl.BlockSpec(memory_space=pltpu.SMEM),
                  pl.BlockSpec((tq, D_HEAD), lambda h, t, qt, kt: (qt[t], q_blk + h)),
                  pl.BlockSpec((tk, D_HEAD), lambda h, t, qt, kt: (kt[t], k_blk + h)),
                  pl.BlockSpec((D_HEAD, tk), lambda h, t, qt, kt: (h, kt[t])),
                  vec, vec, vec, vec,
                  pl.BlockSpec((D_HEAD, 128), lambda h, t, qt, kt: (0, 0))],
        out_specs=pl.BlockSpec((tq, D_HEAD), lambda h, t, qt, kt: (qt[t], h)),
        scratch_shapes=[pltpu.VMEM((2, 1, tq), F32),
                        pltpu.VMEM((2, 1, tq), F32),
                        pltpu.VMEM((2, D_HEAD, tq), F32)],
    )
    return pl.pallas_call(
        functools.partial(_diff_prompt_kernel, tq=tq, tk=tk, lam_init=lam_init),
        grid_spec=grid_spec,
        out_shape=jax.ShapeDtypeStruct((t, GROUP), BF16),
        compiler_params=_params("parallel", "arbitrary"),
        name="diff_prompt",
    )(qi_tab, ki_tab, slopes, obf, obf, vt, *lam_vecs, gd_rep)


def _mixer_out_kernel(r_ref, d_ref, wr_ref, wd_ref, x_ref, gt_ref, sh_ref, sc_ref, gn_ref, xo_ref, h_ref):
    o = (jnp.dot(r_ref[...], wr_ref[...], preferred_element_type=F32)
         + jnp.dot(d_ref[...], wd_ref[...], preferred_element_type=F32))
    x = x_ref[...] + gt_ref[...] * o
    xo_ref[...] = x
    h_ref[...] = (_rms(x, gn_ref[...]) * (1.0 + sc_ref[...]) + sh_ref[...]).astype(BF16)


def _mixer_out(r, dd, w_out_bf, x, gate, shift, scale, g_norm, tm):
    m, d = x.shape
    mod_rows = gate.shape[0]
    mod = pl.BlockSpec((1, d), lambda i: (0, 0)) if mod_rows == 1 else pl.BlockSpec((tm, d), lambda i: (i, 0))
    return pl.pallas_call(
        _mixer_out_kernel,
        grid=(m // tm,),
        in_specs=[pl.BlockSpec((tm, GROUP), lambda i: (i, 0)),
                  pl.BlockSpec((tm, GROUP), lambda i: (i, 0)),
                  pl.BlockSpec((GROUP, d), lambda i: (0, 0)),
                  pl.BlockSpec((GROUP, d), lambda i: (1, 0)),
                  pl.BlockSpec((tm, d), lambda i: (i, 0)),
                  mod, mod, mod,
                  pl.BlockSpec((1, d), lambda i: (0, 0))],
        out_specs=[pl.BlockSpec((tm, d), lambda i: (i, 0)),
                   pl.BlockSpec((tm, d), lambda i: (i, 0))],
        out_shape=[jax.ShapeDtypeStruct((m, d), F32), jax.ShapeDtypeStruct((m, d), BF16)],
        compiler_params=_params("parallel"),
        name="mixer_out",
    )(r, dd, w_out_bf, w_out_bf, x, gate, shift, scale, g_norm.reshape(1, d))


def _ffn_kernel(h_ref, wg_ref, wu_ref, wd_ref, x_ref, gt_ref, o_ref, acc_scr):
    f = pl.program_id(1)
    h = h_ref[...]
    a = (_silu(jnp.dot(h, wg_ref[...], preferred_element_type=F32))
         * jnp.dot(h, wu_ref[...], preferred_element_type=F32)).astype(BF16)
    part = jnp.dot(a, wd_ref[...], preferred_element_type=F32)

    @pl.when(f == 0)
    def _():
        acc_scr[...] = part

    @pl.when(f > 0)
    def _():
        acc_scr[...] += part

    @pl.when(f == pl.num_programs(1) - 1)
    def _():
        o_ref[...] = x_ref[...] + gt_ref[...] * acc_scr[...]


def _ffn(h, wg_bf, wu_bf, wd_bf, x, gate, tm, tf):
    m, d = x.shape
    ff = wg_bf.shape[1]
    mod_rows = gate.shape[0]
    mod = (pl.BlockSpec((1, d), lambda i, f: (0, 0)) if mod_rows == 1
           else pl.BlockSpec((tm, d), lambda i, f: (i, 0)))
    return pl.pallas_call(
        _ffn_kernel,
        grid=(m // tm, ff // tf),
        in_specs=[pl.BlockSpec((tm, d), lambda i, f: (i, 0)),
                  pl.BlockSpec((d, tf), lambda i, f: (0, f)),
                  pl.BlockSpec((d, tf), lambda i, f: (0, f)),
                  pl.BlockSpec((tf, d), lambda i, f: (f, 0)),
                  pl.BlockSpec((tm, d), lambda i, f: (i, 0)),
                  mod],
        out_specs=pl.BlockSpec((tm, d), lambda i, f: (i, 0)),
        out_shape=jax.ShapeDtypeStruct((m, d), F32),
        scratch_shapes=[pltpu.VMEM((tm, d), F32)],
        compiler_params=_params("parallel", "arbitrary"),
        name="ffn",
    )(h, wg_bf, wu_bf, wd_bf, x, gate)


def _ret_sample_kernel(lg_ref, s_ref, q_ref, k_ref, v_ref, g_ref, gn_ref, so_ref, r_ref):
    for h in range(N_HEADS):
        gamma = jnp.exp(jnp.full((1, 1), lg_ref[h], F32))
        state = s_ref[0, h]
        qc = q_ref[0, h]
        kc = k_ref[0, h]
        vr = v_ref[0, h]
        so_ref[0, h] = gamma * state + kc * vr
        qk = jnp.sum(qc * kc, axis=0, keepdims=True)
        o = qk * vr + gamma * jnp.sum(qc * state, axis=0, keepdims=True)
        r_ref[0, h] = _rms(o, gn_ref[h]) * _silu(g_ref[0, h])


def _ret_sample(log_gamma, state, q_col, k_col, v_row, g_row, g_ret_norm):
    b = state.shape[0]
    col = pl.BlockSpec((1, N_HEADS, D_HEAD, 1), lambda i: (i, 0, 0, 0))
    row = pl.BlockSpec((1, N_HEADS, 1, D_HEAD), lambda i: (i, 0, 0, 0))
    st = pl.BlockSpec((1, N_HEADS, D_HEAD, D_HEAD), lambda i: (i, 0, 0, 0))
    return pl.pallas_call(
        _ret_sample_kernel,
        grid=(b,),
        in_specs=[pl.BlockSpec(memory_space=pltpu.SMEM), st, col, col, row, row,
                  pl.BlockSpec((N_HEADS, 1, D_HEAD), lambda i: (0, 0, 0))],
        out_specs=[st, row],
        out_shape=[jax.ShapeDtypeStruct(state.shape, F32),
                   jax.ShapeDtypeStruct((b, N_HEADS, 1, D_HEAD), F32)],
        compiler_params=_params("parallel"),
        name="ret_sample",
    )(log_gamma, state, q_col, k_col, v_row, g_row, g_ret_norm.reshape(N_HEADS, 1, D_HEAD))


PAGE_ROWS = PAGE * 2 * N_HEADS
_V_ROW_SRC = tuple(2 * (j % N_HEADS) for j in range(2 * N_HEADS))


def _diff_sample_kernel(pt_ref, q_ref, kn_ref, vn_ref, slope_ref, diag_ref, lq1_ref, lk1_ref, lq2_ref,
                        lk2_ref, gd_ref, *rest, n_pages, past_len, lam_init):
    k_refs = rest[:n_pages]
    v_refs = rest[n_pages:2 * n_pages]
    o_ref, logit_scr, w_scr, acc_scr = rest[2 * n_pages:]
    s = pl.program_id(1)
    n_ksteps = past_len // (n_pages * PAGE)
    rows = 2 * N_HEADS
    ones = jnp.ones((128, 128), BF16)
    diag = diag_ref[...]

    @pl.when(s == 0)
    def _():
        lane = lax.broadcasted_iota(jnp.int32, (rows, 128), 1)
        l_new = jnp.sum(q_ref[0] * kn_ref[0], axis=-1, keepdims=True)
        logit_scr[:, past_len:] = jnp.where(lane == 0, l_new, -jnp.inf)

    @pl.when(s < n_ksteps)
    def _():
        lane = lax.broadcasted_iota(jnp.int32, (rows, 128), 1)
        q = q_ref[0]
        for i in range(n_pages):
            tok0 = pl.multiple_of((s * n_pages + i) * PAGE, PAGE)
            prod = (k_refs[i][...].reshape(PAGE, rows, 128) * q[None]).reshape(PAGE_ROWS, 128)
            sums = jnp.dot(prod.astype(BF16), ones, preferred_element_type=F32).reshape(PAGE, rows, 128)
            logits = jnp.sum(sums * diag, axis=0)
            dist = (past_len - tok0 - lane).astype(F32)
            logit_scr[:, pl.ds(tok0, PAGE)] = logits - slope_ref[...] * dist

    @pl.when(s == n_ksteps)
    def _():
        logits = logit_scr[...]
        e = jnp.exp(logits - jnp.max(logits, axis=-1, keepdims=True))
        a = e / jnp.sum(e, axis=-1, keepdims=True)
        lam = _lambda(lq1_ref, lk1_ref, lq2_ref, lk2_ref, lam_init)
        dd = a - lam * pltpu.roll(a, rows - 1, 0)
        row = lax.broadcasted_iota(jnp.int32, a.shape, 0)
        w = jnp.zeros_like(a)
        for j in range(rows):
            w = jnp.where(row == j, pltpu.roll(dd, (j - _V_ROW_SRC[j]) % rows, 0), w)
        w_scr[...] = w
        acc_scr[...] = jnp.zeros_like(acc_scr)

    @pl.when(s >= n_ksteps)
    def _():
        acc = acc_scr[...]
        for i in range(n_pages):
            tok0 = pl.multiple_of(((s - n_ksteps) * n_pages + i) * PAGE, PAGE)
            spread = (w_scr[:, pl.ds(tok0, PAGE)][None] * diag).reshape(PAGE_ROWS, 128)
            w_rep = jnp.dot(spread.astype(BF16), ones, preferred_element_type=F32).reshape(PAGE, rows, 128)
            acc = acc + jnp.sum(w_rep * v_refs[i][...].reshape(PAGE, rows, 128), axis=0)
        acc_scr[...] = acc

    @pl.when(s == pl.num_programs(1) - 1)
    def _():
        out = acc_scr[...] + w_scr[:, past_len:][:, :1] * vn_ref[0]
        sq = jnp.sum(out * out, axis=-1, keepdims=True)
        ms = (sq + pltpu.roll(sq, N_HEADS, 0)) / D_HEAD
        o_ref[0] = out * lax.rsqrt(ms + EPS) * gd_ref[...] * (1.0 - lam_init)


def _diff_sample(page_table, q_maps, k_new, v_new, cache_k, cache_v, layer, slope_rows, diag, lam_vecs,
                 gd_rows, lam_init, n_pages):
    b, pages_per_seq = page_table.shape
    past_len = pages_per_seq * PAGE
    n_ksteps = pages_per_seq // n_pages
    rows = 2 * N_HEADS

    def k_spec(i):
        return pl.BlockSpec((None, None, PAGE_ROWS, 128),
                            lambda bi, s, pt: (layer, pt[bi, jnp.minimum(s, n_ksteps - 1) * n_pages + i], 0, 0))

    def v_spec(i):
        return pl.BlockSpec((None, None, PAGE_ROWS, 128),
                            lambda bi, s, pt: (layer, pt[bi, jnp.maximum(s - n_ksteps, 0) * n_pages + i], 0, 0))

    vec = pl.BlockSpec((1, D_MAP), lambda bi, s, pt: (0, 0))
    tok = pl.BlockSpec((1, rows, 128), lambda bi, s, pt: (bi, 0, 0))
    const = pl.BlockSpec((rows, 128), lambda bi, s, pt: (0, 0))
    grid_spec = pltpu.PrefetchScalarGridSpec(
        num_scalar_prefetch=1,
        grid=(b, 2 * n_ksteps),
        in_specs=[tok, tok, tok, const,
                  pl.BlockSpec((PAGE, rows, 128), lambda bi, s, pt: (0, 0, 0)),
                  vec, vec, vec, vec, const]
                 + [k_spec(i) for i in range(n_pages)] + [v_spec(i) for i in range(n_pages)],
        out_specs=tok,
        scratch_shapes=[pltpu.VMEM((rows, past_len + 128), F32),
                        pltpu.VMEM((rows, past_len + 128), F32),
                        pltpu.VMEM((rows, 128), F32)],
    )
    return pl.pallas_call(
        functools.partial(_diff_sample_kernel, n_pages=n_pages, past_len=past_len, lam_init=lam_init),
        grid_spec=grid_spec,
        out_shape=jax.ShapeDtypeStruct((b, rows, 128), F32),
        compiler_params=_params("parallel", "arbitrary"),
        name="diff_sample",
    )(page_table, q_maps, k_new, v_new, slope_rows, diag, *lam_vecs, gd_rows,
      *([cache_k] * n_pages), *([cache_v] * n_pages))


def _dense_tail(r, dd, x, mods, w_out_bf, g_norm_ffn, wg_bf, wu_bf, wd_bf, tm, tf):
    _, _, gt_a, sh_f, sc_f, gt_f = mods
    x, h = _mixer_out(r, dd, w_out_bf, x, gt_a, sh_f, sc_f, g_norm_ffn, tm)
    return _ffn(h, wg_bf, wu_bf, wd_bf, x, gt_f, tm, tf)


def kernel(x_prompt, x_sample, cache_k, cache_v, state_ret, page_table, c_prompt, c_sample,
           w_ada, b_ada, g_norm_mix, w_in, g_qnorm, g_knorm, lambda_q1, lambda_k1, lambda_q2,
           lambda_k2, g_ret_norm, g_diff_norm, w_out, g_norm_ffn, w_gate, w_up, w_down):
    depth = w_in.shape[0]
    n_prompt, t, d = x_prompt.shape
    assert n_prompt == 1
    bs = x_sample.shape[0]
    n_pool = cache_k.shape[1]
    rows = 2 * N_HEADS

    log_gamma = jnp.log1p(-jnp.exp2(-5.0 - jnp.arange(N_HEADS, dtype=F32)))
    slopes = jnp.exp2(-8.0 * (jnp.arange(N_HEADS, dtype=F32) + 1.0) / N_HEADS)
    slope_rows = jnp.broadcast_to(jnp.repeat(slopes, 2)[:, None], (rows, 128))
    diag = jnp.broadcast_to(jnp.eye(PAGE, 128, dtype=F32)[:, None, :], (PAGE, rows, 128))
    q_scale = D_MAP ** -0.5

    pad = (-(bs + 1)) % 8
    c_all = jnp.concatenate([c_sample, c_prompt, jnp.zeros((pad, d), F32)], axis=0)
    mod = _adaln(c_all, w_ada, b_ada)

    w_in_bf, w_out_bf = w_in.astype(BF16), w_out.astype(BF16)
    wg_bf, wu_bf, wd_bf = w_gate.astype(BF16), w_up.astype(BF16), w_down.astype(BF16)
    cache_k2 = cache_k.reshape(depth, n_pool, PAGE_ROWS, 128)
    cache_v2 = (cache_v.reshape(depth, n_pool, PAGE, N_HEADS, 2, 128).transpose(0, 1, 2, 4, 3, 5)
                .reshape(depth, n_pool, PAGE_ROWS, 128))

    def v_rows(a):
        return a.reshape(-1, N_HEADS, 2, 128).transpose(0, 2, 1, 3).reshape(-1, rows, 128)

    x = x_prompt.reshape(t, d)
    xs = x_sample.reshape(bs, d)
    k_p, v_p, s_p, k_s, v_s, s_s = [], [], [], [], [], []
    for l in range(depth):
        lam_init = 0.8 - 0.6 * math.exp(-0.3 * l)
        lam_vecs = [a[l].reshape(1, D_MAP) for a in (lambda_q1, lambda_k1, lambda_q2, lambda_k2)]
        mods_p = [mod[l, bs:bs + 1, i * d:(i + 1) * d] for i in range(N_MOD)]
        mods_s = [mod[l, :bs, i * d:(i + 1) * d] for i in range(N_MOD)]

        obf, g_r, k_d, v_d, vt = _mixer_in(x, mods_p[0], mods_p[1], g_norm_mix[l], w_in_bf[l],
                                           g_qnorm[l], g_knorm[l], 512, q_scale * LOG2E, with_vt=True)
        r, s_fin = _ret_prompt(log_gamma, obf, g_r, g_ret_norm[l], 256)
        dd = _diff_prompt(slopes, obf, vt, lam_vecs, g_diff_norm[l], lam_init, 1024, 512)
        x = _dense_tail(r, dd, x, mods_p, w_out_bf[l], g_norm_ffn[l], wg_bf[l], wu_bf[l], wd_bf[l], 512, 512)
        k_p.append(k_d.reshape(1, t, N_HEADS, 2, D_MAP))
        v_p.append(v_d.reshape(1, t, N_HEADS, D_HEAD))
        s_p.append(s_fin.reshape(1, N_HEADS, D_HEAD, D_HEAD))

        obf, g_r, k_d, v_d = _mixer_in(xs, mods_s[0], mods_s[1], g_norm_mix[l], w_in_bf[l],
                                       g_qnorm[l], g_knorm[l], bs, q_scale)
        q_col = obf[:, :GROUP].astype(F32).reshape(bs, N_HEADS, D_HEAD, 1)
        k_col = obf[:, GROUP:2 * GROUP].astype(F32).reshape(bs, N_HEADS, D_HEAD, 1)
        v_row = obf[:, 2 * GROUP:3 * GROUP].astype(F32).reshape(bs, N_HEADS, 1, D_HEAD)
        s_new, r = _ret_sample(log_gamma, state_ret[l], q_col, k_col, v_row,
                               g_r.reshape(bs, N_HEADS, 1, D_HEAD), g_ret_norm[l])
        q_maps = obf[:, 3 * GROUP:4 * GROUP].astype(F32).reshape(bs, rows, D_MAP)
        gd_rows = jnp.repeat(g_diff_norm[l].reshape(2, 128), N_HEADS, axis=0)
        dd = _diff_sample(page_table, q_maps, k_d.reshape(bs, rows, D_MAP), v_rows(v_d),
                          cache_k2, cache_v2, l, slope_rows, diag, lam_vecs, gd_rows, lam_init, 8)
        dd = dd.reshape(bs, 2, N_HEADS, 128).transpose(0, 2, 1, 3).reshape(bs, GROUP)
        xs = _dense_tail(r.reshape(bs, GROUP).astype(BF16), dd.astype(BF16), xs, mods_s,
                         w_out_bf[l], g_norm_ffn[l], wg_bf[l], wu_bf[l], wd_bf[l], bs, 512)
        k_s.append(k_d.reshape(bs, 1, N_HEADS, 2, D_MAP))
        v_s.append(v_d.reshape(bs, 1, N_HEADS, D_HEAD))
        s_s.append(s_new)

    return (x.reshape(1, t, d), xs.reshape(bs, 1, d), jnp.stack(k_p), jnp.stack(v_p), jnp.stack(s_p),
            jnp.stack(k_s), jnp.stack(v_s), jnp.stack(s_s))
```

```python
import functools
import math

import jax
import jax.numpy as jnp
from jax import lax
from jax.experimental import pallas as pl
from jax.experimental.pallas import tpu as pltpu

F32 = jnp.float32
BF16 = jnp.bfloat16

D_MODEL = 2048
N_HEADS = 4
D_HEAD = 256
D_MAP = 128
GROUP = 1024
N_GROUPS = 7
D_FF = 5632
N_MOD = 6
EPS = 1e-6
PAGE = 128
ROWS = 2 * N_HEADS
VMEM_LIMIT = 56 * 1024 * 1024
LOG2E = math.log2(math.e)

NT_DIMS = (((1,), (1,)), ((), ()))
TN_DIMS = (((0,), (0,)), ((), ()))


def _silu(x):
    return x / (1.0 + jnp.exp(-x))


def _rms(x, g):
    return x * lax.rsqrt(jnp.mean(x * x, axis=-1, keepdims=True) + EPS) * g


def _params(*sem):
    return pltpu.CompilerParams(dimension_semantics=sem, vmem_limit_bytes=VMEM_LIMIT)


def _adaln_kernel(c_ref, w_ref, b_ref, o_ref):
    a = _silu(c_ref[...]).astype(BF16)
    o_ref[0] = jnp.dot(a, w_ref[0].astype(BF16), preferred_element_type=F32) + b_ref[0]


def _adaln(c_all, w_ada, b_ada, tn=1024):
    depth, d, n = w_ada.shape
    rows = c_all.shape[0]
    return pl.pallas_call(
        _adaln_kernel,
        grid=(depth, n // tn),
        in_specs=[pl.BlockSpec((rows, d), lambda l, j: (0, 0)),
                  pl.BlockSpec((1, d, tn), lambda l, j: (l, 0, j)),
                  pl.BlockSpec((1, 1, tn), lambda l, j: (l, 0, j))],
        out_specs=pl.BlockSpec((1, rows, tn), lambda l, j: (l, 0, j)),
        out_shape=jax.ShapeDtypeStruct((depth, rows, n), F32),
        compiler_params=_params("parallel", "parallel"),
        name="adaln",
    )(c_all, w_ada, b_ada.reshape(depth, 1, n))


def _mixer_in_kernel(*refs, q_scale, with_vt, aliased):
    x_ref, sh_ref, sc_ref, gn_ref, w_ref, gq_ref, gk_ref = refs[:7]
    obf_ref, g_ref, k_ref, v_ref = refs[7 + 2 * aliased:11 + 2 * aliased]
    vt_ref = refs[-2] if with_vt else None
    h_scr = refs[-1]
    tm = x_ref.shape[0]
    j = pl.program_id(1)

    @pl.when(j == 0)
    def _():
        y = _rms(x_ref[...], gn_ref[...])
        h_scr[...] = (y * (1.0 + sc_ref[...]) + sh_ref[...]).astype(BF16)

    acc = jnp.dot(h_scr[...], w_ref[...], preferred_element_type=F32)

    @pl.when(j == 0)
    def _():
        obf_ref[...] = acc.astype(BF16)

    @pl.when(j == 1)
    def _():
        obf_ref[...] = (acc * (D_HEAD ** -0.5)).astype(BF16)

    @pl.when(j == 2)
    def _():
        obf_ref[...] = acc.astype(BF16)

    @pl.when(j == 3)
    def _():
        g_ref[...] = acc

    @pl.when(j == 4)
    def _():
        for c in range(GROUP // D_MAP):
            sl = slice(c * D_MAP, (c + 1) * D_MAP)
            obf_ref[:, sl] = (_rms(acc[:, sl], gq_ref[...]) * q_scale).astype(BF16)

    @pl.when(j == 5)
    def _():
        for c in range(GROUP // D_MAP):
            sl = slice(c * D_MAP, (c + 1) * D_MAP)
            kn = _rms(acc[:, sl], gk_ref[...])
            k_ref[pl.ds(c, tm, stride=ROWS), :] = kn
            obf_ref[:, sl] = kn.astype(BF16)

    @pl.when(j == 6)
    def _():
        for c in range(GROUP // D_MAP):
            head, half = divmod(c, 2)
            v_ref[pl.ds(half * N_HEADS + head, tm, stride=ROWS), :] = acc[:, c * D_MAP:(c + 1) * D_MAP]
        obf_ref[...] = acc.astype(BF16)
        if with_vt:
            vt_ref[...] = acc.T.astype(BF16)


def _mixer_in(x, shift, scale, g_norm, w_bf, g_q, g_k, layer, depth, kv_leaves, tm, q_scale, with_vt=False):
    m, d = x.shape
    aliased = kv_leaves is not None
    mod_rows = shift.shape[0]
    mod_blk = (1, d) if mod_rows == 1 else (tm, d)
    mod_map = (lambda i, j: (0, 0)) if mod_rows == 1 else (lambda i, j: (i, 0))
    leaf_spec = pl.BlockSpec((None, tm * ROWS, 128), lambda i, j: (layer, i, 0))
    leaf_shape = jax.ShapeDtypeStruct((depth, m * ROWS, 128), F32)
    vt_spec = [pl.BlockSpec((GROUP, tm), lambda i, j: (0, i))] if with_vt else []
    vt_shape = [jax.ShapeDtypeStruct((GROUP, m), BF16)] if with_vt else []
    any_spec = pl.BlockSpec(memory_space=pl.ANY)
    return pl.pallas_call(
        functools.partial(_mixer_in_kernel, q_scale=q_scale, with_vt=with_vt, aliased=aliased),
        grid=(m // tm, N_GROUPS),
        in_specs=[pl.BlockSpec((tm, d), lambda i, j: (i, 0)),
                  pl.BlockSpec(mod_blk, mod_map),
                  pl.BlockSpec(mod_blk, mod_map),
                  pl.BlockSpec((1, d), lambda i, j: (0, 0)),
                  pl.BlockSpec((d, GROUP), lambda i, j: (0, j)),
                  pl.BlockSpec((1, D_MAP), lambda i, j: (0, 0)),
                  pl.BlockSpec((1, D_MAP), lambda i, j: (0, 0))] + [any_spec, any_spec] * aliased,
        out_specs=[pl.BlockSpec((tm, GROUP), lambda i, j: (i, jnp.where(j >= 3, j - 1, j))),
                   pl.BlockSpec((tm, GROUP), lambda i, j: (i, 0)), leaf_spec, leaf_spec] + vt_spec,
        out_shape=[jax.ShapeDtypeStruct((m, 6 * GROUP), BF16),
                   jax.ShapeDtypeStruct((m, GROUP), F32), leaf_shape, leaf_shape] + vt_shape,
        input_output_aliases={7: 2, 8: 3} if aliased else {},
        scratch_shapes=[pltpu.VMEM((tm, d), BF16)],
        compiler_params=_params("parallel", "arbitrary"),
        name="mixer_in",
    )(x, shift, scale, g_norm.reshape(1, d), w_bf, g_q.reshape(1, D_MAP), g_k.reshape(1, D_MAP),
      *(kv_leaves if aliased else ()))


def _ret_prompt_kernel(lg_ref, q_ref, k_ref, v_ref, g_ref, gn_ref, r_ref, s_ref, dec_scr, *, chunk):
    h = pl.program_id(0)
    c = pl.program_id(1)
    lg = lg_ref[h]

    @pl.when(c == 0)
    def _():
        s_ref[...] = jnp.zeros_like(s_ref)
        i = lax.broadcasted_iota(jnp.int32, (chunk, chunk), 0)
        j = lax.broadcasted_iota(jnp.int32, (chunk, chunk), 1)
        diff = (i - j).astype(F32)
        dec_scr[...] = jnp.where(diff >= 0, jnp.exp(jnp.maximum(diff, 0.0) * lg), 0.0)

    q = q_ref[...]
    k = k_ref[...]
    v = v_ref[...]
    pos = lax.broadcasted_iota(jnp.int32, (chunk, 1), 0).astype(F32)
    q_decay = jnp.exp((pos + 1.0) * lg)
    k_decay = jnp.exp((chunk - 1.0 - pos) * lg)
    chunk_decay = jnp.exp(jnp.full((1, 1), chunk, F32) * lg)
    state = s_ref[0]

    scores = lax.dot_general(q, k, NT_DIMS, preferred_element_type=F32) * dec_scr[...]
    inner = jnp.dot(scores.astype(BF16), v, preferred_element_type=F32)
    cross = jnp.dot(q, state.astype(BF16), preferred_element_type=F32) * q_decay
    k_dec = (k.astype(F32) * k_decay).astype(BF16)
    s_ref[0] = chunk_decay * state + lax.dot_general(k_dec, v, TN_DIMS, preferred_element_type=F32)

    r_ref[...] = (_rms(inner + cross, gn_ref[0]) * _silu(g_ref[...])).astype(BF16)


def _ret_prompt(log_gamma, obf, g_r, g_ret_norm, chunk):
    t = obf.shape[0]
    return pl.pallas_call(
        functools.partial(_ret_prompt_kernel, chunk=chunk),
        grid=(N_HEADS, t // chunk),
        in_specs=[pl.BlockSpec(memory_space=pltpu.SMEM),
                  pl.BlockSpec((chunk, D_HEAD), lambda h, c: (c, h)),
                  pl.BlockSpec((chunk, D_HEAD), lambda h, c: (c, N_HEADS + h)),
                  pl.BlockSpec((chunk, D_HEAD), lambda h, c: (c, 2 * N_HEADS + h)),
                  pl.BlockSpec((chunk, D_HEAD), lambda h, c: (c, h)),
                  pl.BlockSpec((1, 1, D_HEAD), lambda h, c: (h, 0, 0))],
        out_specs=[pl.BlockSpec((chunk, D_HEAD), lambda h, c: (c, h)),
                   pl.BlockSpec((1, D_HEAD, D_HEAD), lambda h, c: (h, 0, 0))],
        out_shape=[jax.ShapeDtypeStruct((t, GROUP), BF16),
                   jax.ShapeDtypeStruct((N_HEADS, D_HEAD, D_HEAD), F32)],
        scratch_shapes=[pltpu.VMEM((chunk, chunk), F32)],
        compiler_params=_params("parallel", "arbitrary"),
        name="ret_prompt",
    )(log_gamma, obf, obf, obf, g_r, g_ret_norm.reshape(N_HEADS, 1, D_HEAD))


def _lambda(lq1_ref, lk1_ref, lq2_ref, lk2_ref, lam_init):
    s1 = jnp.sum(lq1_ref[...] * lk1_ref[...], axis=-1, keepdims=True)
    s2 = jnp.sum(lq2_ref[...] * lk2_ref[...], axis=-1, keepdims=True)
    return jnp.exp(s1) - jnp.exp(s2) + lam_init


def _diff_prompt_kernel(qi_tab, ki_tab, slope_ref, q_ref, k_ref, vt_ref,
                        lq1_ref, lk1_ref, lq2_ref, lk2_ref, gd_ref, o_ref,
                        m_scr, l_scr, acc_scr, *, tq, tk, lam_init):
    h = pl.program_id(0)
    t = pl.program_id(1)
    qi = qi_tab[t]
    ki = ki_tab[t]
    ratio = tq // tk
    slope = slope_ref[h] * LOG2E

    @pl.when(ki == 0)
    def _():
        m_scr[...] = jnp.full_like(m_scr, -jnp.inf)
        l_scr[...] = jnp.zeros_like(l_scr)
        acc_scr[...] = jnp.zeros_like(acc_scr)

    def step(masked):
        q = q_ref[...]
        k = k_ref[...]
        vt = vt_ref[...]
        krow = lax.broadcasted_iota(jnp.int32, (tk, 128), 0)
        bias = pltpu.repeat(slope * (krow + (ki * tk - qi * tq)).astype(F32), tq // 128, axis=1)
        if masked:
            kpos = lax.broadcasted_iota(jnp.int32, (tk, tq), 0) + (ki * tk - qi * tq)
            keep = kpos <= lax.broadcasted_iota(jnp.int32, (tk, tq), 1)
        for m in range(2):
            sl = slice(m * D_MAP, (m + 1) * D_MAP)
            s = lax.dot_general(k[:, sl], q[:, sl], NT_DIMS, preferred_element_type=F32) + bias
            if masked:
                s = jnp.where(keep, s, -jnp.inf)
            m_old = m_scr[m]
            m_new = jnp.maximum(m_old, jnp.max(s, axis=0, keepdims=True))
            alpha = jnp.exp2(m_old - m_new)
            p = jnp.exp2(s - m_new)
            l_scr[m] = alpha * l_scr[m] + jnp.sum(p, axis=0, keepdims=True)
            acc_scr[m] = alpha * acc_scr[m] + jnp.dot(vt, p.astype(BF16), preferred_element_type=F32)
            m_scr[m] = m_new

    @pl.when(ki < qi * ratio)
    def _():
        step(False)

    @pl.when(ki >= qi * ratio)
    def _():
        step(True)

    @pl.when(ki == (qi + 1) * ratio - 1)
    def _():
        lam = _lambda(lq1_ref, lk1_ref, lq2_ref, lk2_ref, lam_init)
        o = acc_scr[0] * (1.0 / l_scr[0]) - lam * (acc_scr[1] * (1.0 / l_scr[1]))
        y = o * lax.rsqrt(jnp.mean(o * o, axis=0, keepdims=True) + EPS)
        y = y * pltpu.repeat(gd_ref[...], tq // 128, axis=1) * (1.0 - lam_init)
        o_ref[...] = y.T.astype(BF16)


def _diff_prompt(slopes, obf, vt, lam_vecs, g_diff_norm, lam_init, tq, tk):
    t = obf.shape[0]
    ratio = tq // tk
    pairs = [(qi, ki) for qi in range(t // tq) for ki in range((qi + 1) * ratio)]
    qi_tab = jnp.asarray([p[0] for p in pairs], jnp.int32)
    ki_tab = jnp.asarray([p[1] for p in pairs], jnp.int32)
    q_blk, k_blk = 3 * N_HEADS, 4 * N_HEADS
    vec = pl.BlockSpec((1, D_MAP), lambda h, t, qt, kt: (0, 0))
    gd_rep = jnp.broadcast_to(g_diff_norm.reshape(D_HEAD, 1), (D_HEAD, 128))
    grid_spec = pltpu.PrefetchScalarGridSpec(
        num_scalar_prefetch=2,
        grid=(N_HEADS, len(pairs)),
        in_specs=[pl.BlockSpec(memory_space=pltpu.SMEM),
                  pl.BlockSpec((tq, D_HEAD), lambda h, t, qt, kt: (qt[t], q_blk + h)),
                  pl.BlockSpec((tk, D_HEAD), lambda h, t, qt, kt: (kt[t], k_blk + h)),
                  pl.BlockSpec((D_HEAD, tk), lambda h, t, qt, kt: (h, kt[t])),
                  vec, vec, vec, vec,
                  pl.BlockSpec((D_HEAD, 128), lambda h, t, qt, kt: (0, 0))],
        out_specs=pl.BlockSpec((tq, D_HEAD), lambda h, t, qt, kt: (qt[t], h)),
        scratch_shapes=[pltpu.VMEM((2, 1, tq), F32),
                        pltpu.VMEM((2, 1, tq), F32),
                        pltpu.VMEM((2, D_HEAD, tq), F32)],
    )
    return pl.pallas_call(
        functools.partial(_diff_prompt_kernel, tq=tq, tk=tk, lam_init=lam_init),
        grid_spec=grid_spec,
        out_shape=jax.ShapeDtypeStruct((t, GROUP), BF16),
        compiler_params=_params("parallel", "arbitrary"),
        name="diff_prompt",
    )(qi_tab, ki_tab, slopes, obf, obf, vt, *lam_vecs, gd_rep)


def _mixer_out_kernel(r_ref, d_ref, wr_ref, wd_ref, x_ref, gt_ref, sh_ref, sc_ref, gn_ref, xo_ref, h_ref):
    o = (jnp.dot(r_ref[...], wr_ref[...], preferred_element_type=F32)
         + jnp.dot(d_ref[...], wd_ref[...], preferred_element_type=F32))
    x = x_ref[...] + gt_ref[...] * o
    xo_ref[...] = x
    h_ref[...] = (_rms(x, gn_ref[...]) * (1.0 + sc_ref[...]) + sh_ref[...]).astype(BF16)


def _mixer_out(r, dd, w_out_bf, x, gate, shift, scale, g_norm, tm):
    m, d = x.shape
    mod_rows = gate.shape[0]
    mod = pl.BlockSpec((1, d), lambda i: (0, 0)) if mod_rows == 1 else pl.BlockSpec((tm, d), lambda i: (i, 0))
    return pl.pallas_call(
        _mixer_out_kernel,
        grid=(m // tm,),
        in_specs=[pl.BlockSpec((tm, GROUP), lambda i: (i, 0)),
                  pl.BlockSpec((tm, GROUP), lambda i: (i, 0)),
                  pl.BlockSpec((GROUP, d), lambda i: (0, 0)),
                  pl.BlockSpec((GROUP, d), lambda i: (1, 0)),
                  pl.BlockSpec((tm, d), lambda i: (i, 0)),
                  mod, mod, mod,
                  pl.BlockSpec((1, d), lambda i: (0, 0))],
        out_specs=[pl.BlockSpec((tm, d), lambda i: (i, 0)),
                   pl.BlockSpec((tm, d), lambda i: (i, 0))],
        out_shape=[jax.ShapeDtypeStruct((m, d), F32), jax.ShapeDtypeStruct((m, d), BF16)],
        compiler_params=_params("parallel"),
        name="mixer_out",
    )(r, dd, w_out_bf, w_out_bf, x, gate, shift, scale, g_norm.reshape(1, d))


def _ffn_kernel(h_ref, wg_ref, wu_ref, wd_ref, x_ref, gt_ref, o_ref, acc_scr):
    f = pl.program_id(1)

    @pl.when(f == 0)
    def _():
        acc_scr[...] = jnp.zeros_like(acc_scr)

    h = h_ref[...]
    half = wg_ref.shape[1] // 2
    part = None
    for c in range(2):
        sl = slice(c * half, (c + 1) * half)
        a = (_silu(jnp.dot(h, wg_ref[:, sl], preferred_element_type=F32))
             * jnp.dot(h, wu_ref[:, sl], preferred_element_type=F32)).astype(BF16)
        p = jnp.dot(a, wd_ref[sl, :], preferred_element_type=F32)
        part = p if part is None else part + p
    acc_scr[...] += part

    @pl.when(f == pl.num_programs(1) - 1)
    def _():
        o_ref[...] = x_ref[...] + gt_ref[...] * acc_scr[...]


def _ffn(h, wg_bf, wu_bf, wd_bf, x, gate, tm, tf):
    m, d = x.shape
    ff = wg_bf.shape[1]
    mod_rows = gate.shape[0]
    mod = (pl.BlockSpec((1, d), lambda i, f: (0, 0)) if mod_rows == 1
           else pl.BlockSpec((tm, d), lambda i, f: (i, 0)))
    return pl.pallas_call(
        _ffn_kernel,
        grid=(m // tm, ff // tf),
        in_specs=[pl.BlockSpec((tm, d), lambda i, f: (i, 0)),
                  pl.BlockSpec((d, tf), lambda i, f: (0, f)),
                  pl.BlockSpec((d, tf), lambda i, f: (0, f)),
                  pl.BlockSpec((tf, d), lambda i, f: (f, 0)),
                  pl.BlockSpec((tm, d), lambda i, f: (i, 0)),
                  mod],
        out_specs=pl.BlockSpec((tm, d), lambda i, f: (i, 0)),
        out_shape=jax.ShapeDtypeStruct((m, d), F32),
        scratch_shapes=[pltpu.VMEM((tm, d), F32)],
        compiler_params=_params("parallel", "arbitrary"),
        name="ffn",
    )(h, wg_bf, wu_bf, wd_bf, x, gate)


def _ret_sample_kernel(*refs, aliased):
    lg_ref, s_ref, q_ref, k_ref, v_ref, g_ref, gn_ref = refs[:7]
    so_ref, r_ref = refs[7 + aliased:]
    ones = jnp.ones((D_HEAD, 128), BF16)
    eye = (lax.broadcasted_iota(jnp.int32, (D_HEAD, D_HEAD), 0)
           == lax.broadcasted_iota(jnp.int32, (D_HEAD, D_HEAD), 1))

    def column(row):
        rep = jnp.dot(jnp.where(eye, row, 0.0).astype(BF16), ones, preferred_element_type=F32)
        return pltpu.repeat(rep, D_HEAD // 128, axis=1)

    for h in range(N_HEADS):
        gamma = jnp.exp(jnp.full((1, 1), lg_ref[h], F32))
        state = s_ref[0, h]
        q = q_ref[0, h]
        k = k_ref[0, h]
        vr = v_ref[0, h]
        so_ref[0, h] = gamma * state + column(k) * vr
        qk = jnp.sum(q * k, axis=-1, keepdims=True)
        o = qk * vr + gamma * jnp.sum(column(q) * state, axis=0, keepdims=True)
        r_ref[0, h] = _rms(o, gn_ref[h]) * _silu(g_ref[0, h])


def _ret_sample(log_gamma, state_all, layer, q_row, k_row, v_row, g_row, g_ret_norm, s_prev):
    depth, b = state_all.shape[:2]
    aliased = s_prev is not None
    row = pl.BlockSpec((1, N_HEADS, 1, D_HEAD), lambda i: (i, 0, 0, 0))
    st = pl.BlockSpec((None, 1, N_HEADS, D_HEAD, D_HEAD), lambda i: (layer, i, 0, 0, 0))
    return pl.pallas_call(
        functools.partial(_ret_sample_kernel, aliased=aliased),
        grid=(b,),
        in_specs=[pl.BlockSpec(memory_space=pltpu.SMEM), st, row, row, row, row,
                  pl.BlockSpec((N_HEADS, 1, D_HEAD), lambda i: (0, 0, 0))]
                 + [pl.BlockSpec(memory_space=pl.ANY)] * aliased,
        out_specs=[st, row],
        out_shape=[jax.ShapeDtypeStruct(state_all.shape, F32),
                   jax.ShapeDtypeStruct((b, N_HEADS, 1, D_HEAD), F32)],
        input_output_aliases={7: 0} if aliased else {},
        compiler_params=_params("parallel"),
        name="ret_sample",
    )(log_gamma, state_all, q_row, k_row, v_row, g_row, g_ret_norm.reshape(N_HEADS, 1, D_HEAD),
      *((s_prev,) if aliased else ()))


PAGE_ROWS = PAGE * 2 * N_HEADS
_V_ROW_SRC = tuple(2 * (j % N_HEADS) for j in range(2 * N_HEADS))


def _diff_sample_kernel(pt_ref, q_ref, kn_ref, vn_ref, slope_ref, diag_ref, lq1_ref, lk1_ref, lq2_ref,
                        lk2_ref, gd_ref, *rest, n_pages, past_len, lam_init):
    k_refs = rest[:n_pages]
    v_refs = rest[n_pages:2 * n_pages]
    o_ref, logit_scr, w_scr, acc_scr = rest[2 * n_pages:]
    s = pl.program_id(1)
    n_ksteps = past_len // (n_pages * PAGE)
    rows = 2 * N_HEADS
    ones = jnp.ones((128, 128), BF16)
    diag = diag_ref[...]

    @pl.when(s == 0)
    def _():
        lane = lax.broadcasted_iota(jnp.int32, (rows, 128), 1)
        l_new = jnp.sum(q_ref[0] * kn_ref[0], axis=-1, keepdims=True)
        logit_scr[:, past_len:] = jnp.where(lane == 0, l_new, -jnp.inf)

    @pl.when(s < n_ksteps)
    def _():
        lane = lax.broadcasted_iota(jnp.int32, (rows, 128), 1)
        q = q_ref[0]
        for i in range(n_pages):
            tok0 = pl.multiple_of((s * n_pages + i) * PAGE, PAGE)
            prod = (k_refs[i][...].reshape(PAGE, rows, 128) * q[None]).reshape(PAGE_ROWS, 128)
            sums = jnp.dot(prod.astype(BF16), ones, preferred_element_type=F32).reshape(PAGE, rows, 128)
            logits = jnp.sum(sums * diag, axis=0)
            dist = (past_len - tok0 - lane).astype(F32)
            logit_scr[:, pl.ds(tok0, PAGE)] = logits - slope_ref[...] * dist

    @pl.when(s == n_ksteps)
    def _():
        logits = logit_scr[...]
        e = jnp.exp(logits - jnp.max(logits, axis=-1, keepdims=True))
        a = e / jnp.sum(e, axis=-1, keepdims=True)
        lam = _lambda(lq1_ref, lk1_ref, lq2_ref, lk2_ref, lam_init)
        dd = a - lam * pltpu.roll(a, rows - 1, 0)
        row = lax.broadcasted_iota(jnp.int32, a.shape, 0)
        w = jnp.zeros_like(a)
        for j in range(rows):
            w = jnp.where(row == j, pltpu.roll(dd, (j - _V_ROW_SRC[j]) % rows, 0), w)
        w_scr[...] = w
        acc_scr[...] = jnp.zeros_like(acc_scr)

    @pl.when(s >= n_ksteps)
    def _():
        acc = acc_scr[...]
        for i in range(n_pages):
            tok0 = pl.multiple_of(((s - n_ksteps) * n_pages + i) * PAGE, PAGE)
            spread = (w_scr[:, pl.ds(tok0, PAGE)][None] * diag).reshape(PAGE_ROWS, 128)
            w_rep = jnp.dot(spread.astype(BF16), ones, preferred_element_type=F32).reshape(PAGE, rows, 128)
            acc = acc + jnp.sum(w_rep * v_refs[i][...].reshape(PAGE, rows, 128), axis=0)
        acc_scr[...] = acc

    @pl.when(s == pl.num_programs(1) - 1)
    def _():
        out = acc_scr[...] + w_scr[:, past_len:][:, :1] * vn_ref[0]
        sq = jnp.sum(out * out, axis=-1, keepdims=True)
        ms = (sq + pltpu.roll(sq, N_HEADS, 0)) / D_HEAD
        o_ref[0] = out * lax.rsqrt(ms + EPS) * gd_ref[...] * (1.0 - lam_init)


def _diff_sample(page_table, q_maps, k_new, v_new, cache_k, cache_v, layer, slope_rows, diag, lam_vecs,
                 gd_rows, lam_init, n_pages):
    b, pages_per_seq = page_table.shape
    past_len = pages_per_seq * PAGE
    n_ksteps = pages_per_seq // n_pages
    rows = 2 * N_HEADS

    def k_spec(i):
        return pl.BlockSpec((None, None, PAGE_ROWS, 128),
                            lambda bi, s, pt: (layer, pt[bi, jnp.minimum(s, n_ksteps - 1) * n_pages + i], 0, 0))

    def v_spec(i):
        return pl.BlockSpec((None, None, PAGE_ROWS, 128),
                            lambda bi, s, pt: (layer, pt[bi, jnp.maximum(s - n_ksteps, 0) * n_pages + i], 0, 0))

    vec = pl.BlockSpec((1, D_MAP), lambda bi, s, pt: (0, 0))
    tok = pl.BlockSpec((1, rows, 128), lambda bi, s, pt: (bi, 0, 0))
    const = pl.BlockSpec((rows, 128), lambda bi, s, pt: (0, 0))
    grid_spec = pltpu.PrefetchScalarGridSpec(
        num_scalar_prefetch=1,
        grid=(b, 2 * n_ksteps),
        in_specs=[tok, tok, tok, const,
                  pl.BlockSpec((PAGE, rows, 128), lambda bi, s, pt: (0, 0, 0)),
                  vec, vec, vec, vec, const]
                 + [k_spec(i) for i in range(n_pages)] + [v_spec(i) for i in range(n_pages)],
        out_specs=tok,
        scratch_shapes=[pltpu.VMEM((rows, past_len + 128), F32),
                        pltpu.VMEM((rows, past_len + 128), F32),
                        pltpu.VMEM((rows, 128), F32)],
    )
    return pl.pallas_call(
        functools.partial(_diff_sample_kernel, n_pages=n_pages, past_len=past_len, lam_init=lam_init),
        grid_spec=grid_spec,
        out_shape=jax.ShapeDtypeStruct((b, rows, 128), F32),
        compiler_params=_params("parallel", "arbitrary"),
        name="diff_sample",
    )(page_table, q_maps, k_new, v_new, slope_rows, diag, *lam_vecs, gd_rows,
      *([cache_k] * n_pages), *([cache_v] * n_pages))


def _dense_tail(r, dd, x, mods, w_out_bf, g_norm_ffn, wg_bf, wu_bf, wd_bf, tm, tf):
    _, _, gt_a, sh_f, sc_f, gt_f = mods
    x, h = _mixer_out(r, dd, w_out_bf, x, gt_a, sh_f, sc_f, g_norm_ffn, tm)
    return _ffn(h, wg_bf, wu_bf, wd_bf, x, gt_f, tm, tf)


def kernel(x_prompt, x_sample, cache_k, cache_v, state_ret, page_table, c_prompt, c_sample,
           w_ada, b_ada, g_norm_mix, w_in, g_qnorm, g_knorm, lambda_q1, lambda_k1, lambda_q2,
           lambda_k2, g_ret_norm, g_diff_norm, w_out, g_norm_ffn, w_gate, w_up, w_down):
    depth = w_in.shape[0]
    n_prompt, t, d = x_prompt.shape
    assert n_prompt == 1
    bs = x_sample.shape[0]
    n_pool = cache_k.shape[1]
    rows = ROWS

    log_gamma = jnp.log1p(-jnp.exp2(-5.0 - jnp.arange(N_HEADS, dtype=F32)))
    slopes = jnp.exp2(-8.0 * (jnp.arange(N_HEADS, dtype=F32) + 1.0) / N_HEADS)
    slope_rows = jnp.broadcast_to(jnp.repeat(slopes, 2)[:, None], (rows, 128))
    diag = jnp.broadcast_to(jnp.eye(PAGE, 128, dtype=F32)[:, None, :], (PAGE, rows, 128))
    q_scale = D_MAP ** -0.5

    pad = (-(bs + 1)) % 8
    c_all = jnp.concatenate([c_sample, c_prompt, jnp.zeros((pad, d), F32)], axis=0)
    mod = _adaln(c_all, w_ada, b_ada)

    w_in_bf, w_out_bf = w_in.astype(BF16), w_out.astype(BF16)
    wg_bf, wu_bf, wd_bf = w_gate.astype(BF16), w_up.astype(BF16), w_down.astype(BF16)
    cache_k2 = cache_k.reshape(depth, n_pool, PAGE_ROWS, 128)
    cache_v2 = (cache_v.reshape(depth, n_pool, PAGE, N_HEADS, 2, 128).transpose(0, 1, 2, 4, 3, 5)
                .reshape(depth, n_pool, PAGE_ROWS, 128))

    x = x_prompt.reshape(t, d)
    xs = x_sample.reshape(bs, d)
    kv_p = kv_s = s_s = None
    s_p = []
    for l in range(depth):
        lam_init = 0.8 - 0.6 * math.exp(-0.3 * l)
        lam_vecs = [a[l].reshape(1, D_MAP) for a in (lambda_q1, lambda_k1, lambda_q2, lambda_k2)]
        mods_p = [mod[l, bs:bs + 1, i * d:(i + 1) * d] for i in range(N_MOD)]
        mods_s = [mod[l, :bs, i * d:(i + 1) * d] for i in range(N_MOD)]

        obf, g_r, k_leaf, v_leaf, vt = _mixer_in(x, mods_p[0], mods_p[1], g_norm_mix[l], w_in_bf[l], g_qnorm[l],
                                                 g_knorm[l], l, depth, kv_p, 512, q_scale * LOG2E, with_vt=True)
        kv_p = (k_leaf, v_leaf)
        r, s_fin = _ret_prompt(log_gamma, obf, g_r, g_ret_norm[l], 256)
        dd = _diff_prompt(slopes, obf, vt, lam_vecs, g_diff_norm[l], lam_init, 1024, 512)
        x = _dense_tail(r, dd, x, mods_p, w_out_bf[l], g_norm_ffn[l], wg_bf[l], wu_bf[l], wd_bf[l], 512, 512)
        s_p.append(s_fin.reshape(1, N_HEADS, D_HEAD, D_HEAD))

        obf, g_r, k_leaf, v_leaf = _mixer_in(xs, mods_s[0], mods_s[1], g_norm_mix[l], w_in_bf[l], g_qnorm[l],
                                             g_knorm[l], l, depth, kv_s, bs, q_scale)
        kv_s = (k_leaf, v_leaf)

        def rows(c):
            return obf[:, c * GROUP:(c + 1) * GROUP].astype(F32).reshape(bs, N_HEADS, 1, D_HEAD)

        s_s, r = _ret_sample(log_gamma, state_ret, l, rows(0), rows(1), rows(2),
                             g_r.reshape(bs, N_HEADS, 1, D_HEAD), g_ret_norm[l], s_s)
        q_maps = obf[:, 3 * GROUP:4 * GROUP].astype(F32).reshape(bs, ROWS, D_MAP)
        gd_rows = jnp.repeat(g_diff_norm[l].reshape(2, 128), N_HEADS, axis=0)
        dd = _diff_sample(page_table, q_maps, k_leaf[l].reshape(bs, ROWS, 128), v_leaf[l].reshape(bs, ROWS, 128),
                          cache_k2, cache_v2, l, slope_rows, diag, lam_vecs, gd_rows, lam_init, 16)
        dd = dd.reshape(bs, 2, N_HEADS, 128).transpose(0, 2, 1, 3).reshape(bs, GROUP)
        xs = _dense_tail(r.reshape(bs, GROUP).astype(BF16), dd.astype(BF16), xs, mods_s,
                         w_out_bf[l], g_norm_ffn[l], wg_bf[l], wu_bf[l], wd_bf[l], bs, 512)

    def k_out(a, n):
        return a.reshape(depth, n, -1, N_HEADS, 2, D_MAP)

    def v_out(a, n):
        return (a.reshape(depth, n, -1, 2, N_HEADS, 128).transpose(0, 1, 2, 4, 3, 5)
                .reshape(depth, n, -1, N_HEADS, D_HEAD))

    return (x.reshape(1, t, d), xs.reshape(bs, 1, d), k_out(kv_p[0], 1), v_out(kv_p[1], 1), jnp.stack(s_p),
            k_out(kv_s[0], bs), v_out(kv_s[1], bs), s_s)
```

```python
import functools
import math

import jax
import jax.numpy as jnp
from jax import lax
from jax.experimental import pallas as pl
from jax.experimental.pallas import tpu as pltpu

F32 = jnp.float32
BF16 = jnp.bfloat16

D_MODEL = 2048
N_HEADS = 4
D_HEAD = 256
D_MAP = 128
GROUP = 1024
N_GROUPS = 7
D_FF = 5632
N_MOD = 6
EPS = 1e-6
PAGE = 128
ROWS = 2 * N_HEADS
VMEM_LIMIT = 56 * 1024 * 1024
LOG2E = math.log2(math.e)

TM_PROMPT = 512
TF = 512
RET_CHUNK = 256
FLASH_TQ = 1024
FLASH_TK = 512
FLASH_SUB = 2
DECODE_PAGES = 16

NT_DIMS = (((1,), (1,)), ((), ()))
TN_DIMS = (((0,), (0,)), ((), ()))


def _silu(x):
    return x / (1.0 + jnp.exp(-x))


def _rms(x, g):
    return x * lax.rsqrt(jnp.mean(x * x, axis=-1, keepdims=True) + EPS) * g


def _params(*sem):
    return pltpu.CompilerParams(dimension_semantics=sem, vmem_limit_bytes=VMEM_LIMIT)


def _adaln_kernel(c_ref, w_ref, b_ref, o_ref):
    a = _silu(c_ref[...]).astype(BF16)
    o_ref[0] = jnp.dot(a, w_ref[0].astype(BF16), preferred_element_type=F32) + b_ref[0]


def _adaln(c_all, w_ada, b_ada, tn=1024):
    depth, d, n = w_ada.shape
    rows = c_all.shape[0]
    return pl.pallas_call(
        _adaln_kernel,
        grid=(depth, n // tn),
        in_specs=[pl.BlockSpec((rows, d), lambda l, j: (0, 0)),
                  pl.BlockSpec((1, d, tn), lambda l, j: (l, 0, j)),
                  pl.BlockSpec((1, 1, tn), lambda l, j: (l, 0, j))],
        out_specs=pl.BlockSpec((1, rows, tn), lambda l, j: (l, 0, j)),
        out_shape=jax.ShapeDtypeStruct((depth, rows, n), F32),
        compiler_params=_params("parallel", "parallel"),
        name="adaln",
    )(c_all, w_ada, b_ada.reshape(depth, 1, n))


def _mixer_in_kernel(*refs, q_scale, with_vt, aliased):
    x_ref, sh_ref, sc_ref, gn_ref, w_ref, gq_ref, gk_ref = refs[:7]
    obf_ref, g_ref, k_ref, v_ref = refs[7 + 2 * aliased:11 + 2 * aliased]
    vt_ref = refs[-2] if with_vt else None
    h_scr = refs[-1]
    tm = x_ref.shape[0]
    j = pl.program_id(1)

    @pl.when(j == 0)
    def _():
        y = _rms(x_ref[...], gn_ref[...])
        h_scr[...] = (y * (1.0 + sc_ref[...]) + sh_ref[...]).astype(BF16)

    acc = jnp.dot(h_scr[...], w_ref[...], preferred_element_type=F32)

    @pl.when(j == 0)
    def _():
        obf_ref[...] = acc.astype(BF16)

    @pl.when(j == 1)
    def _():
        obf_ref[...] = (acc * (D_HEAD ** -0.5)).astype(BF16)

    @pl.when(j == 2)
    def _():
        obf_ref[...] = acc.astype(BF16)

    @pl.when(j == 3)
    def _():
        g_ref[...] = acc

    @pl.when(j == 4)
    def _():
        for c in range(GROUP // D_MAP):
            sl = slice(c * D_MAP, (c + 1) * D_MAP)
            obf_ref[:, sl] = (_rms(acc[:, sl], gq_ref[...]) * q_scale).astype(BF16)

    @pl.when(j == 5)
    def _():
        for c in range(GROUP // D_MAP):
            sl = slice(c * D_MAP, (c + 1) * D_MAP)
            kn = _rms(acc[:, sl], gk_ref[...])
            k_ref[pl.ds(c, tm, stride=ROWS), :] = kn
            obf_ref[:, sl] = kn.astype(BF16)

    @pl.when(j == 6)
    def _():
        for c in range(GROUP // D_MAP):
            head, half = divmod(c, 2)
            v_ref[pl.ds(half * N_HEADS + head, tm, stride=ROWS), :] = acc[:, c * D_MAP:(c + 1) * D_MAP]
        obf_ref[...] = acc.astype(BF16)
        if with_vt:
            vt_ref[...] = acc.T.astype(BF16)


def _mixer_in(x, shift, scale, g_norm, w_bf, g_q, g_k, layer, depth, kv_leaves, tm, q_scale, with_vt=False):
    m, d = x.shape
    aliased = kv_leaves is not None
    mod_rows = shift.shape[0]
    mod_blk = (1, d) if mod_rows == 1 else (tm, d)
    mod_map = (lambda i, j: (0, 0)) if mod_rows == 1 else (lambda i, j: (i, 0))
    leaf_spec = pl.BlockSpec((None, tm * ROWS, 128), lambda i, j: (layer, i, 0))
    leaf_shape = jax.ShapeDtypeStruct((depth, m * ROWS, 128), F32)
    vt_spec = [pl.BlockSpec((GROUP, tm), lambda i, j: (0, i))] if with_vt else []
    vt_shape = [jax.ShapeDtypeStruct((GROUP, m), BF16)] if with_vt else []
    any_spec = pl.BlockSpec(memory_space=pl.ANY)
    return pl.pallas_call(
        functools.partial(_mixer_in_kernel, q_scale=q_scale, with_vt=with_vt, aliased=aliased),
        grid=(m // tm, N_GROUPS),
        in_specs=[pl.BlockSpec((tm, d), lambda i, j: (i, 0)),
                  pl.BlockSpec(mod_blk, mod_map),
                  pl.BlockSpec(mod_blk, mod_map),
                  pl.BlockSpec((1, d), lambda i, j: (0, 0)),
                  pl.BlockSpec((None, d, GROUP), lambda i, j: (layer, 0, j)),
                  pl.BlockSpec((1, D_MAP), lambda i, j: (0, 0)),
                  pl.BlockSpec((1, D_MAP), lambda i, j: (0, 0))] + [any_spec, any_spec] * aliased,
        out_specs=[pl.BlockSpec((tm, GROUP), lambda i, j: (i, jnp.where(j >= 3, j - 1, j))),
                   pl.BlockSpec((tm, GROUP), lambda i, j: (i, 0)), leaf_spec, leaf_spec] + vt_spec,
        out_shape=[jax.ShapeDtypeStruct((m, 6 * GROUP), BF16),
                   jax.ShapeDtypeStruct((m, GROUP), F32), leaf_shape, leaf_shape] + vt_shape,
        input_output_aliases={7: 2, 8: 3} if aliased else {},
        scratch_shapes=[pltpu.VMEM((tm, d), BF16)],
        compiler_params=_params("parallel", "arbitrary"),
        name="mixer_in",
    )(x, shift, scale, g_norm.reshape(1, d), w_bf, g_q.reshape(1, D_MAP), g_k.reshape(1, D_MAP),
      *(kv_leaves if aliased else ()))


def _ret_prompt_kernel(lg_ref, q_ref, k_ref, v_ref, g_ref, gn_ref, r_ref, s_ref, dec_scr, *, chunk):
    c = pl.program_id(0)
    row = lax.broadcasted_iota(jnp.int32, (chunk, chunk), 0)
    col = lax.broadcasted_iota(jnp.int32, (chunk, chunk), 1)
    pos = lax.broadcasted_iota(jnp.int32, (chunk, 1), 0).astype(F32)

    @pl.when(c == 0)
    def _():
        s_ref[...] = jnp.zeros_like(s_ref)
        diff = (row - col).astype(F32)
        for h in range(N_HEADS):
            dec_scr[h] = jnp.where(diff >= 0, jnp.exp(jnp.maximum(diff, 0.0) * lg_ref[h]), 0.0)

    for h in range(N_HEADS):
        lg = lg_ref[h]
        sl = slice(h * D_HEAD, (h + 1) * D_HEAD)
        q = q_ref[:, sl]
        k = k_ref[:, sl]
        v = v_ref[:, sl]
        q_decay = jnp.exp((pos + 1.0) * lg)
        k_decay = jnp.exp((chunk - 1.0 - pos) * lg)
        chunk_decay = jnp.exp(jnp.full((1, 1), chunk, F32) * lg)
        state = s_ref[h]

        scores = lax.dot_general(q, k, NT_DIMS, preferred_element_type=F32) * dec_scr[h]
        inner = jnp.dot(scores.astype(BF16), v, preferred_element_type=F32)
        cross = jnp.dot(q, state.astype(BF16), preferred_element_type=F32) * q_decay
        k_dec = (k.astype(F32) * k_decay).astype(BF16)
        s_ref[h] = chunk_decay * state + lax.dot_general(k_dec, v, TN_DIMS, preferred_element_type=F32)

        r_ref[:, sl] = (_rms(inner + cross, gn_ref[h]) * _silu(g_ref[:, sl])).astype(BF16)


def _ret_prompt(log_gamma, obf, g_r, g_ret_norm, chunk):
    t = obf.shape[0]
    return pl.pallas_call(
        functools.partial(_ret_prompt_kernel, chunk=chunk),
        grid=(t // chunk,),
        in_specs=[pl.BlockSpec(memory_space=pltpu.SMEM),
                  pl.BlockSpec((chunk, GROUP), lambda c: (c, 0)),
                  pl.BlockSpec((chunk, GROUP), lambda c: (c, 1)),
                  pl.BlockSpec((chunk, GROUP), lambda c: (c, 2)),
                  pl.BlockSpec((chunk, GROUP), lambda c: (c, 0)),
                  pl.BlockSpec((N_HEADS, 1, D_HEAD), lambda c: (0, 0, 0))],
        out_specs=[pl.BlockSpec((chunk, GROUP), lambda c: (c, 0)),
                   pl.BlockSpec((N_HEADS, D_HEAD, D_HEAD), lambda c: (0, 0, 0))],
        out_shape=[jax.ShapeDtypeStruct((t, GROUP), BF16),
                   jax.ShapeDtypeStruct((N_HEADS, D_HEAD, D_HEAD), F32)],
        scratch_shapes=[pltpu.VMEM((N_HEADS, chunk, chunk), F32)],
        compiler_params=_params("arbitrary"),
        name="ret_prompt",
    )(log_gamma, obf, obf, obf, g_r, g_ret_norm.reshape(N_HEADS, 1, D_HEAD))


def _lambda(lq1_ref, lk1_ref, lq2_ref, lk2_ref, lam_init):
    s1 = jnp.sum(lq1_ref[...] * lk1_ref[...], axis=-1, keepdims=True)
    s2 = jnp.sum(lq2_ref[...] * lk2_ref[...], axis=-1, keepdims=True)
    return jnp.exp(s1) - jnp.exp(s2) + lam_init


def _diff_prompt_kernel(qi_tab, ki_tab, slope_ref, q_ref, k_ref, vt_ref,
                        lq1_ref, lk1_ref, lq2_ref, lk2_ref, gd_ref, o_ref,
                        m_scr, l_scr, acc_scr, *, tq, tk, n_sub, lam_init):
    h = pl.program_id(0)
    t = pl.program_id(1)
    qi = qi_tab[t]
    ki = ki_tab[t]
    tkb = tk * n_sub
    ratio = tq // tkb
    slope = slope_ref[h] * LOG2E

    @pl.when(ki == 0)
    def _():
        m_scr[...] = jnp.full_like(m_scr, -jnp.inf)
        l_scr[...] = jnp.zeros_like(l_scr)
        acc_scr[...] = jnp.zeros_like(acc_scr)

    def tile(masked, sub):
        q = q_ref[...]
        k = k_ref[sub * tk:(sub + 1) * tk, :]
        vt = vt_ref[:, sub * tk:(sub + 1) * tk]
        koff = ki * tkb + sub * tk - qi * tq
        krow = lax.broadcasted_iota(jnp.int32, (tk, 128), 0)
        bias = pltpu.repeat(slope * (krow + koff).astype(F32), tq // 128, axis=1)
        if masked:
            kpos = lax.broadcasted_iota(jnp.int32, (tk, tq), 0) + koff
            keep = kpos <= lax.broadcasted_iota(jnp.int32, (tk, tq), 1)
        for m in range(2):
            sl = slice(m * D_MAP, (m + 1) * D_MAP)
            s = lax.dot_general(k[:, sl], q[:, sl], NT_DIMS, preferred_element_type=F32) + bias
            if masked:
                s = jnp.where(keep, s, -jnp.inf)
            m_old = m_scr[m]
            m_new = jnp.maximum(m_old, jnp.max(s, axis=0, keepdims=True))
            alpha = jnp.exp2(m_old - m_new)
            p = jnp.exp2(s - m_new)
            l_scr[m] = alpha * l_scr[m] + jnp.sum(p, axis=0, keepdims=True)
            acc_scr[m] = alpha * acc_scr[m] + jnp.dot(vt, p.astype(BF16), preferred_element_type=F32)
            m_scr[m] = m_new

    @pl.when(ki < qi * ratio)
    def _():
        for sub in range(n_sub):
            tile(False, sub)

    @pl.when(ki >= qi * ratio)
    def _():
        for sub in range(n_sub):
            tile(True, sub)

    @pl.when(ki == (qi + 1) * ratio - 1)
    def _():
        lam = _lambda(lq1_ref, lk1_ref, lq2_ref, lk2_ref, lam_init)
        o = acc_scr[0] * (1.0 / l_scr[0]) - lam * (acc_scr[1] * (1.0 / l_scr[1]))
        y = o * lax.rsqrt(jnp.mean(o * o, axis=0, keepdims=True) + EPS)
        y = y * pltpu.repeat(gd_ref[...], tq // 128, axis=1) * (1.0 - lam_init)
        o_ref[...] = y.T.astype(BF16)


def _diff_prompt(slopes, obf, vt, lam_vecs, g_diff_norm, lam_init, tq, tk, n_sub):
    t = obf.shape[0]
    tkb = tk * n_sub
    ratio = tq // tkb
    pairs = [(qi, ki) for qi in range(t // tq) for ki in range((qi + 1) * ratio)]
    qi_tab = jnp.asarray([p[0] for p in pairs], jnp.int32)
    ki_tab = jnp.asarray([p[1] for p in pairs], jnp.int32)
    q_blk, k_blk = 3 * N_HEADS, 4 * N_HEADS
    vec = pl.BlockSpec((1, D_MAP), lambda h, t, qt, kt: (0, 0))
    gd_rep = jnp.broadcast_to(g_diff_norm.reshape(D_HEAD, 1), (D_HEAD, 128))
    grid_spec = pltpu.PrefetchScalarGridSpec(
        num_scalar_prefetch=2,
        grid=(N_HEADS, len(pairs)),
        in_specs=[pl.BlockSpec(memory_space=pltpu.SMEM),
                  pl.BlockSpec((tq, D_HEAD), lambda h, t, qt, kt: (qt[t], q_blk + h)),
                  pl.BlockSpec((tkb, D_HEAD), lambda h, t, qt, kt: (kt[t], k_blk + h)),
                  pl.BlockSpec((D_HEAD, tkb), lambda h, t, qt, kt: (h, kt[t])),
                  vec, vec, vec, vec,
                  pl.BlockSpec((D_HEAD, 128), lambda h, t, qt, kt: (0, 0))],
        out_specs=pl.BlockSpec((tq, D_HEAD), lambda h, t, qt, kt: (qt[t], h)),
        scratch_shapes=[pltpu.VMEM((2, 1, tq), F32),
                        pltpu.VMEM((2, 1, tq), F32),
                        pltpu.VMEM((2, D_HEAD, tq), F32)],
    )
    return pl.pallas_call(
        functools.partial(_diff_prompt_kernel, tq=tq, tk=tk, n_sub=n_sub, lam_init=lam_init),
        grid_spec=grid_spec,
        out_shape=jax.ShapeDtypeStruct((t, GROUP), BF16),
        compiler_params=_params("parallel", "arbitrary"),
        name="diff_prompt",
    )(qi_tab, ki_tab, slopes, obf, obf, vt, *lam_vecs, gd_rep)


def _mixer_out_kernel(r_ref, d_ref, wr_ref, wd_ref, x_ref, gt_ref, sh_ref, sc_ref, gn_ref, xo_ref, h_ref):
    o = (jnp.dot(r_ref[...], wr_ref[...], preferred_element_type=F32)
         + jnp.dot(d_ref[...], wd_ref[...], preferred_element_type=F32))
    x = x_ref[...] + gt_ref[...] * o
    xo_ref[...] = x
    h_ref[...] = (_rms(x, gn_ref[...]) * (1.0 + sc_ref[...]) + sh_ref[...]).astype(BF16)


def _mixer_out(r, dd, w_out_bf, layer, x, gate, shift, scale, g_norm, tm):
    m, d = x.shape
    mod_rows = gate.shape[0]
    mod = pl.BlockSpec((1, d), lambda i: (0, 0)) if mod_rows == 1 else pl.BlockSpec((tm, d), lambda i: (i, 0))
    return pl.pallas_call(
        _mixer_out_kernel,
        grid=(m // tm,),
        in_specs=[pl.BlockSpec((tm, GROUP), lambda i: (i, 0)),
                  pl.BlockSpec((tm, GROUP), lambda i: (i, 0)),
                  pl.BlockSpec((None, GROUP, d), lambda i: (layer, 0, 0)),
                  pl.BlockSpec((None, GROUP, d), lambda i: (layer, 1, 0)),
                  pl.BlockSpec((tm, d), lambda i: (i, 0)),
                  mod, mod, mod,
                  pl.BlockSpec((1, d), lambda i: (0, 0))],
        out_specs=[pl.BlockSpec((tm, d), lambda i: (i, 0)),
                   pl.BlockSpec((tm, d), lambda i: (i, 0))],
        out_shape=[jax.ShapeDtypeStruct((m, d), F32), jax.ShapeDtypeStruct((m, d), BF16)],
        compiler_params=_params("parallel"),
        name="mixer_out",
    )(r, dd, w_out_bf, w_out_bf, x, gate, shift, scale, g_norm.reshape(1, d))


def _ffn_kernel(h_ref, wg_ref, wu_ref, wd_ref, x_ref, gt_ref, o_ref, acc_scr):
    f = pl.program_id(1)

    @pl.when(f == 0)
    def _():
        acc_scr[...] = jnp.zeros_like(acc_scr)

    h = h_ref[...]
    half = wg_ref.shape[1] // 2
    part = None
    for c in range(2):
        sl = slice(c * half, (c + 1) * half)
        a = (_silu(jnp.dot(h, wg_ref[:, sl], preferred_element_type=F32))
             * jnp.dot(h, wu_ref[:, sl], preferred_element_type=F32)).astype(BF16)
        p = jnp.dot(a, wd_ref[sl, :], preferred_element_type=F32)
        part = p if part is None else part + p
    acc_scr[...] += part

    @pl.when(f == pl.num_programs(1) - 1)
    def _():
        o_ref[...] = x_ref[...] + gt_ref[...] * acc_scr[...]


def _ffn(h, wg_bf, wu_bf, wd_bf, layer, x, gate, tm, tf):
    m, d = x.shape
    ff = wg_bf.shape[2]
    mod_rows = gate.shape[0]
    mod = (pl.BlockSpec((1, d), lambda i, f: (0, 0)) if mod_rows == 1
           else pl.BlockSpec((tm, d), lambda i, f: (i, 0)))
    return pl.pallas_call(
        _ffn_kernel,
        grid=(m // tm, ff // tf),
        in_specs=[pl.BlockSpec((tm, d), lambda i, f: (i, 0)),
                  pl.BlockSpec((None, d, tf), lambda i, f: (layer, 0, f)),
                  pl.BlockSpec((None, d, tf), lambda i, f: (layer, 0, f)),
                  pl.BlockSpec((None, tf, d), lambda i, f: (layer, f, 0)),
                  pl.BlockSpec((tm, d), lambda i, f: (i, 0)),
                  mod],
        out_specs=pl.BlockSpec((tm, d), lambda i, f: (i, 0)),
        out_shape=jax.ShapeDtypeStruct((m, d), F32),
        scratch_shapes=[pltpu.VMEM((tm, d), F32)],
        compiler_params=_params("parallel", "arbitrary"),
        name="ffn",
    )(h, wg_bf, wu_bf, wd_bf, x, gate)


def _ret_sample_kernel(*refs, aliased):
    lg_ref, s_ref, q_ref, k_ref, v_ref, g_ref, gn_ref = refs[:7]
    so_ref, r_ref = refs[7 + aliased:]
    ones = jnp.ones((D_HEAD, 128), BF16)
    eye = (lax.broadcasted_iota(jnp.int32, (D_HEAD, D_HEAD), 0)
           == lax.broadcasted_iota(jnp.int32, (D_HEAD, D_HEAD), 1))

    def column(row):
        rep = jnp.dot(jnp.where(eye, row, 0.0).astype(BF16), ones, preferred_element_type=F32)
        return pltpu.repeat(rep, D_HEAD // 128, axis=1)

    for h in range(N_HEADS):
        gamma = jnp.exp(jnp.full((1, 1), lg_ref[h], F32))
        state = s_ref[0, h]
        q = q_ref[0, h]
        k = k_ref[0, h]
        vr = v_ref[0, h]
        so_ref[0, h] = gamma * state + column(k) * vr
        qk = jnp.sum(q * k, axis=-1, keepdims=True)
        o = qk * vr + gamma * jnp.sum(column(q) * state, axis=0, keepdims=True)
        r_ref[0, h] = _rms(o, gn_ref[h]) * _silu(g_ref[0, h])


def _ret_sample(log_gamma, state_all, layer, q_row, k_row, v_row, g_row, g_ret_norm, s_prev):
    depth, b = state_all.shape[:2]
    aliased = s_prev is not None
    row = pl.BlockSpec((1, N_HEADS, 1, D_HEAD), lambda i: (i, 0, 0, 0))
    st = pl.BlockSpec((None, 1, N_HEADS, D_HEAD, D_HEAD), lambda i: (layer, i, 0, 0, 0))
    return pl.pallas_call(
        functools.partial(_ret_sample_kernel, aliased=aliased),
        grid=(b,),
        in_specs=[pl.BlockSpec(memory_space=pltpu.SMEM), st, row, row, row, row,
                  pl.BlockSpec((N_HEADS, 1, D_HEAD), lambda i: (0, 0, 0))]
                 + [pl.BlockSpec(memory_space=pl.ANY)] * aliased,
        out_specs=[st, row],
        out_shape=[jax.ShapeDtypeStruct(state_all.shape, F32),
                   jax.ShapeDtypeStruct((b, N_HEADS, 1, D_HEAD), F32)],
        input_output_aliases={7: 0} if aliased else {},
        compiler_params=_params("parallel"),
        name="ret_sample",
    )(log_gamma, state_all, q_row, k_row, v_row, g_row, g_ret_norm.reshape(N_HEADS, 1, D_HEAD),
      *((s_prev,) if aliased else ()))


PAGE_ROWS = PAGE * 2 * N_HEADS
_V_ROW_SRC = tuple(2 * (j % N_HEADS) for j in range(2 * N_HEADS))


def _diff_sample_kernel(pt_ref, q_ref, kn_ref, vn_ref, slope_ref, diag_ref, lq1_ref, lk1_ref, lq2_ref,
                        lk2_ref, gd_ref, *rest, n_pages, past_len, lam_init):
    k_refs = rest[:n_pages]
    v_refs = rest[n_pages:2 * n_pages]
    o_ref, logit_scr, w_scr, acc_scr = rest[2 * n_pages:]
    s = pl.program_id(1)
    n_ksteps = past_len // (n_pages * PAGE)
    rows = 2 * N_HEADS
    ones = jnp.ones((128, 128), BF16)
    diag = diag_ref[...]

    @pl.when(s == 0)
    def _():
        lane = lax.broadcasted_iota(jnp.int32, (rows, 128), 1)
        l_new = jnp.sum(q_ref[0] * kn_ref[0], axis=-1, keepdims=True)
        logit_scr[:, past_len:] = jnp.where(lane == 0, l_new, -jnp.inf)

    @pl.when(s < n_ksteps)
    def _():
        lane = lax.broadcasted_iota(jnp.int32, (rows, 128), 1)
        q = q_ref[0]
        for i in range(n_pages):
            tok0 = pl.multiple_of((s * n_pages + i) * PAGE, PAGE)
            prod = (k_refs[i][...].reshape(PAGE, rows, 128) * q[None]).reshape(PAGE_ROWS, 128)
            sums = jnp.dot(prod.astype(BF16), ones, preferred_element_type=F32).reshape(PAGE, rows, 128)
            logits = jnp.sum(sums * diag, axis=0)
            dist = (past_len - tok0 - lane).astype(F32)
            logit_scr[:, pl.ds(tok0, PAGE)] = logits - slope_ref[...] * dist

    @pl.when(s == n_ksteps)
    def _():
        logits = logit_scr[...]
        e = jnp.exp(logits - jnp.max(logits, axis=-1, keepdims=True))
        a = e / jnp.sum(e, axis=-1, keepdims=True)
        lam = _lambda(lq1_ref, lk1_ref, lq2_ref, lk2_ref, lam_init)
        dd = a - lam * pltpu.roll(a, rows - 1, 0)
        row = lax.broadcasted_iota(jnp.int32, a.shape, 0)
        w = jnp.zeros_like(a)
        for j in range(rows):
            w = jnp.where(row == j, pltpu.roll(dd, (j - _V_ROW_SRC[j]) % rows, 0), w)
        w_scr[...] = w
        acc_scr[...] = jnp.zeros_like(acc_scr)

    @pl.when(s >= n_ksteps)
    def _():
        acc = acc_scr[...]
        for i in range(n_pages):
            tok0 = pl.multiple_of(((s - n_ksteps) * n_pages + i) * PAGE, PAGE)
            spread = (w_scr[:, pl.ds(tok0, PAGE)][None] * diag).reshape(PAGE_ROWS, 128)
            w_rep = jnp.dot(spread.astype(BF16), ones, preferred_element_type=F32).reshape(PAGE, rows, 128)
            acc = acc + jnp.sum(w_rep * v_refs[i][...].reshape(PAGE, rows, 128), axis=0)
        acc_scr[...] = acc

    @pl.when(s == pl.num_programs(1) - 1)
    def _():
        out = acc_scr[...] + w_scr[:, past_len:][:, :1] * vn_ref[0]
        sq = jnp.sum(out * out, axis=-1, keepdims=True)
        ms = (sq + pltpu.roll(sq, N_HEADS, 0)) / D_HEAD
        o_ref[0] = out * lax.rsqrt(ms + EPS) * gd_ref[...] * (1.0 - lam_init)


def _diff_sample(page_table, q_maps, k_new, v_new, cache_k, cache_v, layer, slope_rows, diag, lam_vecs,
                 gd_rows, lam_init, n_pages):
    b, pages_per_seq = page_table.shape
    past_len = pages_per_seq * PAGE
    n_ksteps = pages_per_seq // n_pages
    rows = 2 * N_HEADS

    def k_spec(i):
        return pl.BlockSpec((None, None, PAGE_ROWS, 128),
                            lambda bi, s, pt: (layer, pt[bi, jnp.minimum(s, n_ksteps - 1) * n_pages + i], 0, 0))

    def v_spec(i):
        return pl.BlockSpec((None, None, PAGE_ROWS, 128),
                            lambda bi, s, pt: (layer, pt[bi, jnp.maximum(s - n_ksteps, 0) * n_pages + i], 0, 0))

    vec = pl.BlockSpec((1, D_MAP), lambda bi, s, pt: (0, 0))
    tok = pl.BlockSpec((1, rows, 128), lambda bi, s, pt: (bi, 0, 0))
    const = pl.BlockSpec((rows, 128), lambda bi, s, pt: (0, 0))
    grid_spec = pltpu.PrefetchScalarGridSpec(
        num_scalar_prefetch=1,
        grid=(b, 2 * n_ksteps),
        in_specs=[tok, tok, tok, const,
                  pl.BlockSpec((PAGE, rows, 128), lambda bi, s, pt: (0, 0, 0)),
                  vec, vec, vec, vec, const]
                 + [k_spec(i) for i in range(n_pages)] + [v_spec(i) for i in range(n_pages)],
        out_specs=tok,
        scratch_shapes=[pltpu.VMEM((rows, past_len + 128), F32),
                        pltpu.VMEM((rows, past_len + 128), F32),
                        pltpu.VMEM((rows, 128), F32)],
    )
    return pl.pallas_call(
        functools.partial(_diff_sample_kernel, n_pages=n_pages, past_len=past_len, lam_init=lam_init),
        grid_spec=grid_spec,
        out_shape=jax.ShapeDtypeStruct((b, rows, 128), F32),
        compiler_params=_params("parallel", "arbitrary"),
        name="diff_sample",
    )(page_table, q_maps, k_new, v_new, slope_rows, diag, *lam_vecs, gd_rows,
      *([cache_k] * n_pages), *([cache_v] * n_pages))


def _dense_tail(r, dd, x, mods, layer, w_out_bf, g_norm_ffn, wg_bf, wu_bf, wd_bf, tm, tf):
    _, _, gt_a, sh_f, sc_f, gt_f = mods
    x, h = _mixer_out(r, dd, w_out_bf, layer, x, gt_a, sh_f, sc_f, g_norm_ffn, tm)
    return _ffn(h, wg_bf, wu_bf, wd_bf, layer, x, gt_f, tm, tf)


def kernel(x_prompt, x_sample, cache_k, cache_v, state_ret, page_table, c_prompt, c_sample,
           w_ada, b_ada, g_norm_mix, w_in, g_qnorm, g_knorm, lambda_q1, lambda_k1, lambda_q2,
           lambda_k2, g_ret_norm, g_diff_norm, w_out, g_norm_ffn, w_gate, w_up, w_down):
    depth = w_in.shape[0]
    n_prompt, t, d = x_prompt.shape
    assert n_prompt == 1
    bs = x_sample.shape[0]
    n_pool = cache_k.shape[1]
    rows = ROWS

    log_gamma = jnp.log1p(-jnp.exp2(-5.0 - jnp.arange(N_HEADS, dtype=F32)))
    slopes = jnp.exp2(-8.0 * (jnp.arange(N_HEADS, dtype=F32) + 1.0) / N_HEADS)
    slope_rows = jnp.broadcast_to(jnp.repeat(slopes, 2)[:, None], (rows, 128))
    diag = jnp.broadcast_to(jnp.eye(PAGE, 128, dtype=F32)[:, None, :], (PAGE, rows, 128))
    q_scale = D_MAP ** -0.5

    pad = (-(bs + 1)) % 8
    c_all = jnp.concatenate([c_sample, c_prompt, jnp.zeros((pad, d), F32)], axis=0)
    mod = _adaln(c_all, w_ada, b_ada)

    w_in_bf, w_out_bf = w_in.astype(BF16), w_out.astype(BF16)
    wg_bf, wu_bf, wd_bf = w_gate.astype(BF16), w_up.astype(BF16), w_down.astype(BF16)
    cache_k2 = cache_k.reshape(depth, n_pool, PAGE_ROWS, 128)
    cache_v2 = (cache_v.reshape(depth, n_pool, PAGE, N_HEADS, 2, 128).transpose(0, 1, 2, 4, 3, 5)
                .reshape(depth, n_pool, PAGE_ROWS, 128))

    x = x_prompt.reshape(t, d)
    xs = x_sample.reshape(bs, d)
    kv_p = kv_s = s_s = None
    s_p = []
    for l in range(depth):
        lam_init = 0.8 - 0.6 * math.exp(-0.3 * l)
        lam_vecs = [a[l].reshape(1, D_MAP) for a in (lambda_q1, lambda_k1, lambda_q2, lambda_k2)]
        mods_p = [mod[l, bs:bs + 1, i * d:(i + 1) * d] for i in range(N_MOD)]
        mods_s = [mod[l, :bs, i * d:(i + 1) * d] for i in range(N_MOD)]

        obf, g_r, k_leaf, v_leaf, vt = _mixer_in(x, mods_p[0], mods_p[1], g_norm_mix[l], w_in_bf, g_qnorm[l],
                                                 g_knorm[l], l, depth, kv_p, TM_PROMPT, q_scale * LOG2E, with_vt=True)
        kv_p = (k_leaf, v_leaf)
        r, s_fin = _ret_prompt(log_gamma, obf, g_r, g_ret_norm[l], RET_CHUNK)
        dd = _diff_prompt(slopes, obf, vt, lam_vecs, g_diff_norm[l], lam_init, FLASH_TQ, FLASH_TK, FLASH_SUB)
        x = _dense_tail(r, dd, x, mods_p, l, w_out_bf, g_norm_ffn[l], wg_bf, wu_bf, wd_bf, TM_PROMPT, TF)
        s_p.append(s_fin.reshape(1, N_HEADS, D_HEAD, D_HEAD))

        obf, g_r, k_leaf, v_leaf = _mixer_in(xs, mods_s[0], mods_s[1], g_norm_mix[l], w_in_bf, g_qnorm[l],
                                             g_knorm[l], l, depth, kv_s, bs, q_scale)
        kv_s = (k_leaf, v_leaf)

        def rows(c):
            return obf[:, c * GROUP:(c + 1) * GROUP].astype(F32).reshape(bs, N_HEADS, 1, D_HEAD)

        s_s, r = _ret_sample(log_gamma, state_ret, l, rows(0), rows(1), rows(2),
                             g_r.reshape(bs, N_HEADS, 1, D_HEAD), g_ret_norm[l], s_s)
        q_maps = obf[:, 3 * GROUP:4 * GROUP].astype(F32).reshape(bs, ROWS, D_MAP)
        gd_rows = jnp.repeat(g_diff_norm[l].reshape(2, 128), N_HEADS, axis=0)
        dd = _diff_sample(page_table, q_maps, k_leaf[l].reshape(bs, ROWS, 128), v_leaf[l].reshape(bs, ROWS, 128),
                          cache_k2, cache_v2, l, slope_rows, diag, lam_vecs, gd_rows, lam_init, DECODE_PAGES)
        dd = dd.reshape(bs, 2, N_HEADS, 128).transpose(0, 2, 1, 3).reshape(bs, GROUP)
        xs = _dense_tail(r.reshape(bs, GROUP).astype(BF16), dd.astype(BF16), xs, mods_s,
                         l, w_out_bf, g_norm_ffn[l], wg_bf, wu_bf, wd_bf, bs, TF)

    def k_out(a, n):
        return a.reshape(depth, n, -1, N_HEADS, 2, D_MAP)

    def v_out(a, n):
        return (a.reshape(depth, n, -1, 2, N_HEADS, 128).transpose(0, 1, 2, 4, 3, 5)
                .reshape(depth, n, -1, N_HEADS, D_HEAD))

    return (x.reshape(1, t, d), xs.reshape(bs, 1, d), k_out(kv_p[0], 1), v_out(kv_p[1], 1), jnp.stack(s_p),
            k_out(kv_s[0], bs), v_out(kv_s[1], bs), s_s)
```

```python
import functools
import math

import jax
import jax.numpy as jnp
from jax import lax
from jax.experimental import pallas as pl
from jax.experimental.pallas import tpu as pltpu

F32 = jnp.float32
BF16 = jnp.bfloat16

D_MODEL = 2048
N_HEADS = 4
D_HEAD = 256
D_MAP = 128
GROUP = 1024
N_GROUPS = 7
D_FF = 5632
N_MOD = 6
EPS = 1e-6
PAGE = 128
ROWS = 2 * N_HEADS
VMEM_LIMIT = 56 * 1024 * 1024
LOG2E = math.log2(math.e)

TM_PROMPT = 512
TF = 512
RET_CHUNK = 256
FLASH_TQ = 1024
FLASH_TK = 512
FLASH_SUB = 2
DECODE_PAGES = 16

NT_DIMS = (((1,), (1,)), ((), ()))
TN_DIMS = (((0,), (0,)), ((), ()))


def _silu(x):
    return x / (1.0 + jnp.exp(-x))


def _rms(x, g):
    return x * lax.rsqrt(jnp.mean(x * x, axis=-1, keepdims=True) + EPS) * g


def _params(*sem):
    return pltpu.CompilerParams(dimension_semantics=sem, vmem_limit_bytes=VMEM_LIMIT)


def _adaln_kernel(c_ref, w_ref, b_ref, o_ref):
    a = _silu(c_ref[...]).astype(BF16)
    o_ref[0] = jnp.dot(a, w_ref[0].astype(BF16), preferred_element_type=F32) + b_ref[0]


def _adaln(c_all, w_ada, b_ada, tn=1024):
    depth, d, n = w_ada.shape
    rows = c_all.shape[0]
    return pl.pallas_call(
        _adaln_kernel,
        grid=(depth, n // tn),
        in_specs=[pl.BlockSpec((rows, d), lambda l, j: (0, 0)),
                  pl.BlockSpec((1, d, tn), lambda l, j: (l, 0, j)),
                  pl.BlockSpec((1, 1, tn), lambda l, j: (l, 0, j))],
        out_specs=pl.BlockSpec((1, rows, tn), lambda l, j: (l, 0, j)),
        out_shape=jax.ShapeDtypeStruct((depth, rows, n), F32),
        compiler_params=_params("parallel", "parallel"),
        name="adaln",
    )(c_all, w_ada, b_ada.reshape(depth, 1, n))


def _mixer_in_kernel(*refs, q_scale, with_vt, aliased):
    x_ref, sh_ref, sc_ref, gn_ref, w_ref, gq_ref, gk_ref = refs[:7]
    obf_ref, g_ref, k_ref, v_ref = refs[7 + 2 * aliased:11 + 2 * aliased]
    vt_ref = refs[-2] if with_vt else None
    h_scr = refs[-1]
    tm = x_ref.shape[0]
    j = pl.program_id(1)

    @pl.when(j == 0)
    def _():
        y = _rms(x_ref[...], gn_ref[...])
        h_scr[...] = (y * (1.0 + sc_ref[...]) + sh_ref[...]).astype(BF16)

    acc = jnp.dot(h_scr[...], w_ref[...], preferred_element_type=F32)

    @pl.when(j == 0)
    def _():
        obf_ref[...] = acc.astype(BF16)

    @pl.when(j == 1)
    def _():
        obf_ref[...] = (acc * (D_HEAD ** -0.5)).astype(BF16)

    @pl.when(j == 2)
    def _():
        obf_ref[...] = acc.astype(BF16)

    @pl.when(j == 3)
    def _():
        g_ref[...] = acc

    @pl.when(j == 4)
    def _():
        for c in range(GROUP // D_MAP):
            sl = slice(c * D_MAP, (c + 1) * D_MAP)
            obf_ref[:, sl] = (_rms(acc[:, sl], gq_ref[...]) * q_scale).astype(BF16)

    @pl.when(j == 5)
    def _():
        for c in range(GROUP // D_MAP):
            sl = slice(c * D_MAP, (c + 1) * D_MAP)
            kn = _rms(acc[:, sl], gk_ref[...])
            k_ref[pl.ds(c, tm, stride=ROWS), :] = kn
            obf_ref[:, sl] = kn.astype(BF16)

    @pl.when(j == 6)
    def _():
        for c in range(GROUP // D_MAP):
            head, half = divmod(c, 2)
            v_ref[pl.ds(half * N_HEADS + head, tm, stride=ROWS), :] = acc[:, c * D_MAP:(c + 1) * D_MAP]
        obf_ref[...] = acc.astype(BF16)
        if with_vt:
            vt_ref[...] = acc.T.astype(BF16)


def _mixer_in(x, shift, scale, g_norm, w_bf, g_q, g_k, layer, depth, kv_leaves, tm, q_scale, with_vt=False):
    m, d = x.shape
    aliased = kv_leaves is not None
    mod_rows = shift.shape[0]
    mod_blk = (1, d) if mod_rows == 1 else (tm, d)
    mod_map = (lambda i, j: (0, 0)) if mod_rows == 1 else (lambda i, j: (i, 0))
    leaf_spec = pl.BlockSpec((None, tm * ROWS, 128), lambda i, j: (layer, i, 0))
    leaf_shape = jax.ShapeDtypeStruct((depth, m * ROWS, 128), F32)
    vt_spec = [pl.BlockSpec((GROUP, tm), lambda i, j: (0, i))] if with_vt else []
    vt_shape = [jax.ShapeDtypeStruct((GROUP, m), BF16)] if with_vt else []
    any_spec = pl.BlockSpec(memory_space=pl.ANY)
    return pl.pallas_call(
        functools.partial(_mixer_in_kernel, q_scale=q_scale, with_vt=with_vt, aliased=aliased),
        grid=(m // tm, N_GROUPS),
        in_specs=[pl.BlockSpec((tm, d), lambda i, j: (i, 0)),
                  pl.BlockSpec(mod_blk, mod_map),
                  pl.BlockSpec(mod_blk, mod_map),
                  pl.BlockSpec((1, d), lambda i, j: (0, 0)),
                  pl.BlockSpec((None, d, GROUP), lambda i, j: (layer, 0, j)),
                  pl.BlockSpec((1, D_MAP), lambda i, j: (0, 0)),
                  pl.BlockSpec((1, D_MAP), lambda i, j: (0, 0))] + [any_spec, any_spec] * aliased,
        out_specs=[pl.BlockSpec((tm, GROUP), lambda i, j: (i, jnp.where(j >= 3, j - 1, j))),
                   pl.BlockSpec((tm, GROUP), lambda i, j: (i, 0)), leaf_spec, leaf_spec] + vt_spec,
        out_shape=[jax.ShapeDtypeStruct((m, 6 * GROUP), BF16),
                   jax.ShapeDtypeStruct((m, GROUP), F32), leaf_shape, leaf_shape] + vt_shape,
        input_output_aliases={7: 2, 8: 3} if aliased else {},
        scratch_shapes=[pltpu.VMEM((tm, d), BF16)],
        compiler_params=_params("parallel", "arbitrary"),
        name="mixer_in",
    )(x, shift, scale, g_norm.reshape(1, d), w_bf, g_q.reshape(1, D_MAP), g_k.reshape(1, D_MAP),
      *(kv_leaves if aliased else ()))


def _ret_prompt_kernel(lg_ref, q_ref, k_ref, v_ref, g_ref, gn_ref, r_ref, s_ref, dec_scr, *, chunk):
    c = pl.program_id(0)
    row = lax.broadcasted_iota(jnp.int32, (chunk, chunk), 0)
    col = lax.broadcasted_iota(jnp.int32, (chunk, chunk), 1)
    pos = lax.broadcasted_iota(jnp.int32, (chunk, 1), 0).astype(F32)

    @pl.when(c == 0)
    def _():
        s_ref[...] = jnp.zeros_like(s_ref)
        diff = (row - col).astype(F32)
        for h in range(N_HEADS):
            dec_scr[h] = jnp.where(diff >= 0, jnp.exp(jnp.maximum(diff, 0.0) * lg_ref[h]), 0.0)

    for h in range(N_HEADS):
        lg = lg_ref[h]
        sl = slice(h * D_HEAD, (h + 1) * D_HEAD)
        q = q_ref[:, sl]
        k = k_ref[:, sl]
        v = v_ref[:, sl]
        q_decay = jnp.exp((pos + 1.0) * lg)
        k_decay = jnp.exp((chunk - 1.0 - pos) * lg)
        chunk_decay = jnp.exp(jnp.full((1, 1), chunk, F32) * lg)
        state = s_ref[h]

        scores = lax.dot_general(q, k, NT_DIMS, preferred_element_type=F32) * dec_scr[h]
        inner = jnp.dot(scores.astype(BF16), v, preferred_element_type=F32)
        cross = jnp.dot(q, state.astype(BF16), preferred_element_type=F32) * q_decay
        k_dec = (k.astype(F32) * k_decay).astype(BF16)
        s_ref[h] = chunk_decay * state + lax.dot_general(k_dec, v, TN_DIMS, preferred_element_type=F32)

        r_ref[:, sl] = (_rms(inner + cross, gn_ref[h]) * _silu(g_ref[:, sl])).astype(BF16)


def _ret_prompt(log_gamma, obf, g_r, g_ret_norm, chunk):
    t = obf.shape[0]
    return pl.pallas_call(
        functools.partial(_ret_prompt_kernel, chunk=chunk),
        grid=(t // chunk,),
        in_specs=[pl.BlockSpec(memory_space=pltpu.SMEM),
                  pl.BlockSpec((chunk, GROUP), lambda c: (c, 0)),
                  pl.BlockSpec((chunk, GROUP), lambda c: (c, 1)),
                  pl.BlockSpec((chunk, GROUP), lambda c: (c, 2)),
                  pl.BlockSpec((chunk, GROUP), lambda c: (c, 0)),
                  pl.BlockSpec((N_HEADS, 1, D_HEAD), lambda c: (0, 0, 0))],
        out_specs=[pl.BlockSpec((chunk, GROUP), lambda c: (c, 0)),
                   pl.BlockSpec((N_HEADS, D_HEAD, D_HEAD), lambda c: (0, 0, 0))],
        out_shape=[jax.ShapeDtypeStruct((t, GROUP), BF16),
                   jax.ShapeDtypeStruct((N_HEADS, D_HEAD, D_HEAD), F32)],
        scratch_shapes=[pltpu.VMEM((N_HEADS, chunk, chunk), F32)],
        compiler_params=_params("arbitrary"),
        name="ret_prompt",
    )(log_gamma, obf, obf, obf, g_r, g_ret_norm.reshape(N_HEADS, 1, D_HEAD))


def _lambda(lq1_ref, lk1_ref, lq2_ref, lk2_ref, lam_init):
    s1 = jnp.sum(lq1_ref[...] * lk1_ref[...], axis=-1, keepdims=True)
    s2 = jnp.sum(lq2_ref[...] * lk2_ref[...], axis=-1, keepdims=True)
    return jnp.exp(s1) - jnp.exp(s2) + lam_init


PAGE_ROWS = PAGE * ROWS
_V_ROW_SRC = tuple(2 * (j % N_HEADS) for j in range(ROWS))


def _attention_kernel(qi_tab, ki_tab, pt_ref, slope_ref, q_ref, k_ref, vt_ref, lq1_ref, lk1_ref, lq2_ref,
                      lk2_ref, gd_ref, qs_ref, kn_ref, vn_ref, slope_rows_ref, diag_ref, gd_rows_ref, *rest,
                      tq, tk, n_sub, lam_init, n_pages, past_len, n_seq):
    k_pages = rest[:n_pages]
    v_pages = rest[n_pages:2 * n_pages]
    o_ref, od_ref, m_scr, l_scr, acc_scr, logit_scr, w_scr, dacc_scr = rest[2 * n_pages:]
    h = pl.program_id(0)
    t = pl.program_id(1)

    qi = qi_tab[t]
    ki = ki_tab[t]
    tkb = tk * n_sub
    ratio = tq // tkb
    slope = slope_ref[h] * LOG2E

    @pl.when(ki == 0)
    def _():
        m_scr[...] = jnp.full_like(m_scr, -jnp.inf)
        l_scr[...] = jnp.zeros_like(l_scr)
        acc_scr[...] = jnp.zeros_like(acc_scr)

    def tile(masked, sub):
        q = q_ref[...]
        k = k_ref[sub * tk:(sub + 1) * tk, :]
        vt = vt_ref[:, sub * tk:(sub + 1) * tk]
        koff = ki * tkb + sub * tk - qi * tq
        krow = lax.broadcasted_iota(jnp.int32, (tk, 128), 0)
        bias = pltpu.repeat(slope * (krow + koff).astype(F32), tq // 128, axis=1)
        if masked:
            kpos = lax.broadcasted_iota(jnp.int32, (tk, tq), 0) + koff
            keep = kpos <= lax.broadcasted_iota(jnp.int32, (tk, tq), 1)
        for m in range(2):
            sl = slice(m * D_MAP, (m + 1) * D_MAP)
            s = lax.dot_general(k[:, sl], q[:, sl], NT_DIMS, preferred_element_type=F32) + bias
            if masked:
                s = jnp.where(keep, s, -jnp.inf)
            m_old = m_scr[m]
            m_new = jnp.maximum(m_old, jnp.max(s, axis=0, keepdims=True))
            alpha = jnp.exp2(m_old - m_new)
            p = jnp.exp2(s - m_new)
            l_scr[m] = alpha * l_scr[m] + jnp.sum(p, axis=0, keepdims=True)
            acc_scr[m] = alpha * acc_scr[m] + jnp.dot(vt, p.astype(BF16), preferred_element_type=F32)
            m_scr[m] = m_new

    @pl.when(ki < qi * ratio)
    def _():
        for sub in range(n_sub):
            tile(False, sub)

    @pl.when(ki >= qi * ratio)
    def _():
        for sub in range(n_sub):
            tile(True, sub)

    @pl.when(ki == (qi + 1) * ratio - 1)
    def _():
        lam = _lambda(lq1_ref, lk1_ref, lq2_ref, lk2_ref, lam_init)
        o = acc_scr[0] * (1.0 / l_scr[0]) - lam * (acc_scr[1] * (1.0 / l_scr[1]))
        y = o * lax.rsqrt(jnp.mean(o * o, axis=0, keepdims=True) + EPS)
        y = y * pltpu.repeat(gd_ref[...], tq // 128, axis=1) * (1.0 - lam_init)
        o_ref[...] = y.T.astype(BF16)

    n_chunks = past_len // (n_pages * PAGE)
    g = h * pl.num_programs(1) + t
    phase = g % n_chunks
    seq_k = g // n_chunks
    seq_v = seq_k - 1
    k_live = seq_k < n_seq
    v_live = jnp.logical_and(seq_v >= 0, seq_v < n_seq)
    lane = lax.broadcasted_iota(jnp.int32, (ROWS, 128), 1)

    def lane_sum(x):
        return jnp.dot(x.astype(BF16), jnp.ones((128, 128), BF16), preferred_element_type=F32)

    @pl.when(jnp.logical_and(v_live, phase == 0))
    def _():
        logits = logit_scr[...]
        e = jnp.exp(logits - jnp.max(logits, axis=-1, keepdims=True))
        a = e / jnp.sum(e, axis=-1, keepdims=True)
        lam = _lambda(lq1_ref, lk1_ref, lq2_ref, lk2_ref, lam_init)
        dd = a - lam * pltpu.roll(a, ROWS - 1, 0)
        row = lax.broadcasted_iota(jnp.int32, a.shape, 0)
        w = jnp.zeros_like(a)
        for j in range(ROWS):
            w = jnp.where(row == j, pltpu.roll(dd, (j - _V_ROW_SRC[j]) % ROWS, 0), w)
        w_scr[...] = w
        dacc_scr[...] = jnp.zeros_like(dacc_scr)

    @pl.when(jnp.logical_and(k_live, phase == 0))
    def _():
        l_new = jnp.sum(qs_ref[0] * kn_ref[0], axis=-1, keepdims=True)
        logit_scr[:, past_len:] = jnp.where(lane == 0, l_new, -jnp.inf)

    @pl.when(k_live)
    def _():
        q = qs_ref[0]
        diag = diag_ref[...]
        for i in range(n_pages):
            tok0 = pl.multiple_of((phase * n_pages + i) * PAGE, PAGE)
            prod = (k_pages[i][...].reshape(PAGE, ROWS, 128) * q[None]).reshape(PAGE_ROWS, 128)
            sums = lane_sum(prod).reshape(PAGE, ROWS, 128)
            logits = jnp.sum(sums * diag, axis=0)
            dist = (past_len - tok0 - lane).astype(F32)
            logit_scr[:, pl.ds(tok0, PAGE)] = logits - slope_rows_ref[...] * dist

    @pl.when(v_live)
    def _():
        diag = diag_ref[...]
        acc = dacc_scr[...]
        for i in range(n_pages):
            tok0 = pl.multiple_of((phase * n_pages + i) * PAGE, PAGE)
            spread = (w_scr[:, pl.ds(tok0, PAGE)][None] * diag).reshape(PAGE_ROWS, 128)
            w_rep = lane_sum(spread).reshape(PAGE, ROWS, 128)
            acc = acc + jnp.sum(w_rep * v_pages[i][...].reshape(PAGE, ROWS, 128), axis=0)
        dacc_scr[...] = acc

    @pl.when(jnp.logical_and(v_live, phase == n_chunks - 1))
    def _():
        out = dacc_scr[...] + w_scr[:, past_len:][:, :1] * vn_ref[0]
        sq = jnp.sum(out * out, axis=-1, keepdims=True)
        ms = (sq + pltpu.roll(sq, N_HEADS, 0)) / D_HEAD
        od_ref[0] = out * lax.rsqrt(ms + EPS) * gd_rows_ref[...] * (1.0 - lam_init)


def _attention(slopes, obf, vt, lam_vecs, g_diff_norm, lam_init, page_table, q_maps, k_new, v_new,
               cache_k, cache_v, layer, tq, tk, n_sub, n_pages):
    t = obf.shape[0]
    tkb = tk * n_sub
    ratio = tq // tkb
    pairs = [(qi, ki) for qi in range(t // tq) for ki in range((qi + 1) * ratio)]
    n_steps = len(pairs)
    qi_tab = jnp.asarray([p[0] for p in pairs], jnp.int32)
    ki_tab = jnp.asarray([p[1] for p in pairs], jnp.int32)
    q_blk, k_blk = 3 * N_HEADS, 4 * N_HEADS
    gd_rep = jnp.broadcast_to(g_diff_norm.reshape(D_HEAD, 1), (D_HEAD, 128))

    n_seq, pages_per_seq = page_table.shape
    past_len = pages_per_seq * PAGE
    n_chunks = pages_per_seq // n_pages
    last = n_seq * n_chunks - 1
    assert N_HEADS * n_steps >= (n_seq + 1) * n_chunks, "not enough grid steps to stream the cache"
    slope_rows = jnp.broadcast_to(jnp.repeat(slopes, 2)[:, None], (ROWS, 128))
    diag = jnp.broadcast_to(jnp.eye(PAGE, 128, dtype=F32)[:, None, :], (PAGE, ROWS, 128))
    gd_rows = jnp.repeat(g_diff_norm.reshape(2, 128), N_HEADS, axis=0)

    def step(h, t):
        return h * n_steps + t

    def k_spec(i):
        def index(h, t, qt, kt, pt):
            gk = jnp.minimum(step(h, t), last)
            return (layer, pt[gk // n_chunks, (gk % n_chunks) * n_pages + i], 0, 0)
        return pl.BlockSpec((None, None, PAGE_ROWS, 128), index)

    def v_spec(i):
        def index(h, t, qt, kt, pt):
            gv = jnp.clip(step(h, t) - n_chunks, 0, last)
            return (layer, pt[gv // n_chunks, (gv % n_chunks) * n_pages + i], 0, 0)
        return pl.BlockSpec((None, None, PAGE_ROWS, 128), index)

    vec = pl.BlockSpec((1, D_MAP), lambda h, t, qt, kt, pt: (0, 0))
    tok_k = pl.BlockSpec((1, ROWS, 128),
                         lambda h, t, qt, kt, pt: (jnp.minimum(step(h, t) // n_chunks, n_seq - 1), 0, 0))
    tok_v = pl.BlockSpec((1, ROWS, 128),
                         lambda h, t, qt, kt, pt: (jnp.clip(step(h, t) // n_chunks - 1, 0, n_seq - 1), 0, 0))
    const = pl.BlockSpec((ROWS, 128), lambda h, t, qt, kt, pt: (0, 0))
    grid_spec = pltpu.PrefetchScalarGridSpec(
        num_scalar_prefetch=3,
        grid=(N_HEADS, n_steps),
        in_specs=[pl.BlockSpec(memory_space=pltpu.SMEM),
                  pl.BlockSpec((tq, D_HEAD), lambda h, t, qt, kt, pt: (qt[t], q_blk + h)),
                  pl.BlockSpec((tkb, D_HEAD), lambda h, t, qt, kt, pt: (kt[t], k_blk + h)),
                  pl.BlockSpec((D_HEAD, tkb), lambda h, t, qt, kt, pt: (h, kt[t])),
                  vec, vec, vec, vec,
                  pl.BlockSpec((D_HEAD, 128), lambda h, t, qt, kt, pt: (0, 0)),
                  tok_k, tok_k, tok_v, const,
                  pl.BlockSpec((PAGE, ROWS, 128), lambda h, t, qt, kt, pt: (0, 0, 0)),
                  const]
                 + [k_spec(i) for i in range(n_pages)] + [v_spec(i) for i in range(n_pages)],
        out_specs=[pl.BlockSpec((tq, D_HEAD), lambda h, t, qt, kt, pt: (qt[t], h)), tok_v],
        scratch_shapes=[pltpu.VMEM((2, 1, tq), F32),
                        pltpu.VMEM((2, 1, tq), F32),
                        pltpu.VMEM((2, D_HEAD, tq), F32),
                        pltpu.VMEM((ROWS, past_len + 128), F32),
                        pltpu.VMEM((ROWS, past_len + 128), F32),
                        pltpu.VMEM((ROWS, 128), F32)],
    )
    return pl.pallas_call(
        functools.partial(_attention_kernel, tq=tq, tk=tk, n_sub=n_sub, lam_init=lam_init, n_pages=n_pages,
                          past_len=past_len, n_seq=n_seq),
        grid_spec=grid_spec,
        out_shape=[jax.ShapeDtypeStruct((t, GROUP), BF16), jax.ShapeDtypeStruct((n_seq, ROWS, 128), F32)],
        compiler_params=_params("arbitrary", "arbitrary"),
        name="attention",
    )(qi_tab, ki_tab, page_table, slopes, obf, obf, vt, *lam_vecs, gd_rep, q_maps, k_new, v_new, slope_rows,
      diag, gd_rows, *([cache_k] * n_pages), *([cache_v] * n_pages))


def _mixer_out_kernel(r_ref, d_ref, wr_ref, wd_ref, x_ref, gt_ref, sh_ref, sc_ref, gn_ref, xo_ref, h_ref):
    o = (jnp.dot(r_ref[...], wr_ref[...], preferred_element_type=F32)
         + jnp.dot(d_ref[...], wd_ref[...], preferred_element_type=F32))
    x = x_ref[...] + gt_ref[...] * o
    xo_ref[...] = x
    h_ref[...] = (_rms(x, gn_ref[...]) * (1.0 + sc_ref[...]) + sh_ref[...]).astype(BF16)


def _mixer_out(r, dd, w_out_bf, layer, x, gate, shift, scale, g_norm, tm):
    m, d = x.shape
    mod_rows = gate.shape[0]
    mod = pl.BlockSpec((1, d), lambda i: (0, 0)) if mod_rows == 1 else pl.BlockSpec((tm, d), lambda i: (i, 0))
    return pl.pallas_call(
        _mixer_out_kernel,
        grid=(m // tm,),
        in_specs=[pl.BlockSpec((tm, GROUP), lambda i: (i, 0)),
                  pl.BlockSpec((tm, GROUP), lambda i: (i, 0)),
                  pl.BlockSpec((None, GROUP, d), lambda i: (layer, 0, 0)),
                  pl.BlockSpec((None, GROUP, d), lambda i: (layer, 1, 0)),
                  pl.BlockSpec((tm, d), lambda i: (i, 0)),
                  mod, mod, mod,
                  pl.BlockSpec((1, d), lambda i: (0, 0))],
        out_specs=[pl.BlockSpec((tm, d), lambda i: (i, 0)),
                   pl.BlockSpec((tm, d), lambda i: (i, 0))],
        out_shape=[jax.ShapeDtypeStruct((m, d), F32), jax.ShapeDtypeStruct((m, d), BF16)],
        compiler_params=_params("parallel"),
        name="mixer_out",
    )(r, dd, w_out_bf, w_out_bf, x, gate, shift, scale, g_norm.reshape(1, d))


def _ffn_kernel(h_ref, wg_ref, wu_ref, wd_ref, x_ref, gt_ref, o_ref, acc_scr):
    f = pl.program_id(1)

    @pl.when(f == 0)
    def _():
        acc_scr[...] = jnp.zeros_like(acc_scr)

    h = h_ref[...]
    half = wg_ref.shape[1] // 2
    part = None
    for c in range(2):
        sl = slice(c * half, (c + 1) * half)
        a = (_silu(jnp.dot(h, wg_ref[:, sl], preferred_element_type=F32))
             * jnp.dot(h, wu_ref[:, sl], preferred_element_type=F32)).astype(BF16)
        p = jnp.dot(a, wd_ref[sl, :], preferred_element_type=F32)
        part = p if part is None else part + p
    acc_scr[...] += part

    @pl.when(f == pl.num_programs(1) - 1)
    def _():
        o_ref[...] = x_ref[...] + gt_ref[...] * acc_scr[...]


def _ffn(h, wg_bf, wu_bf, wd_bf, layer, x, gate, tm, tf):
    m, d = x.shape
    ff = wg_bf.shape[2]
    mod_rows = gate.shape[0]
    mod = (pl.BlockSpec((1, d), lambda i, f: (0, 0)) if mod_rows == 1
           else pl.BlockSpec((tm, d), lambda i, f: (i, 0)))
    return pl.pallas_call(
        _ffn_kernel,
        grid=(m // tm, ff // tf),
        in_specs=[pl.BlockSpec((tm, d), lambda i, f: (i, 0)),
                  pl.BlockSpec((None, d, tf), lambda i, f: (layer, 0, f)),
                  pl.BlockSpec((None, d, tf), lambda i, f: (layer, 0, f)),
                  pl.BlockSpec((None, tf, d), lambda i, f: (layer, f, 0)),
                  pl.BlockSpec((tm, d), lambda i, f: (i, 0)),
                  mod],
        out_specs=pl.BlockSpec((tm, d), lambda i, f: (i, 0)),
        out_shape=jax.ShapeDtypeStruct((m, d), F32),
        scratch_shapes=[pltpu.VMEM((tm, d), F32)],
        compiler_params=_params("parallel", "arbitrary"),
        name="ffn",
    )(h, wg_bf, wu_bf, wd_bf, x, gate)


def _ret_sample_kernel(*refs, aliased):
    lg_ref, s_ref, q_ref, k_ref, v_ref, g_ref, gn_ref = refs[:7]
    so_ref, r_ref = refs[7 + aliased:]
    ones = jnp.ones((D_HEAD, 128), BF16)
    eye = (lax.broadcasted_iota(jnp.int32, (D_HEAD, D_HEAD), 0)
           == lax.broadcasted_iota(jnp.int32, (D_HEAD, D_HEAD), 1))

    def column(row):
        rep = jnp.dot(jnp.where(eye, row, 0.0).astype(BF16), ones, preferred_element_type=F32)
        return pltpu.repeat(rep, D_HEAD // 128, axis=1)

    for h in range(N_HEADS):
        gamma = jnp.exp(jnp.full((1, 1), lg_ref[h], F32))
        state = s_ref[0, h]
        q = q_ref[0, h]
        k = k_ref[0, h]
        vr = v_ref[0, h]
        so_ref[0, h] = gamma * state + column(k) * vr
        qk = jnp.sum(q * k, axis=-1, keepdims=True)
        o = qk * vr + gamma * jnp.sum(column(q) * state, axis=0, keepdims=True)
        r_ref[0, h] = _rms(o, gn_ref[h]) * _silu(g_ref[0, h])


def _ret_sample(log_gamma, state_all, layer, q_row, k_row, v_row, g_row, g_ret_norm, s_prev):
    depth, b = state_all.shape[:2]
    aliased = s_prev is not None
    row = pl.BlockSpec((1, N_HEADS, 1, D_HEAD), lambda i: (i, 0, 0, 0))
    st = pl.BlockSpec((None, 1, N_HEADS, D_HEAD, D_HEAD), lambda i: (layer, i, 0, 0, 0))
    return pl.pallas_call(
        functools.partial(_ret_sample_kernel, aliased=aliased),
        grid=(b,),
        in_specs=[pl.BlockSpec(memory_space=pltpu.SMEM), st, row, row, row, row,
                  pl.BlockSpec((N_HEADS, 1, D_HEAD), lambda i: (0, 0, 0))]
                 + [pl.BlockSpec(memory_space=pl.ANY)] * aliased,
        out_specs=[st, row],
        out_shape=[jax.ShapeDtypeStruct(state_all.shape, F32),
                   jax.ShapeDtypeStruct((b, N_HEADS, 1, D_HEAD), F32)],
        input_output_aliases={7: 0} if aliased else {},
        compiler_params=_params("parallel"),
        name="ret_sample",
    )(log_gamma, state_all, q_row, k_row, v_row, g_row, g_ret_norm.reshape(N_HEADS, 1, D_HEAD),
      *((s_prev,) if aliased else ()))


def _dense_tail(r, dd, x, mods, layer, w_out_bf, g_norm_ffn, wg_bf, wu_bf, wd_bf, tm, tf):
    _, _, gt_a, sh_f, sc_f, gt_f = mods
    x, h = _mixer_out(r, dd, w_out_bf, layer, x, gt_a, sh_f, sc_f, g_norm_ffn, tm)
    return _ffn(h, wg_bf, wu_bf, wd_bf, layer, x, gt_f, tm, tf)


def kernel(x_prompt, x_sample, cache_k, cache_v, state_ret, page_table, c_prompt, c_sample,
           w_ada, b_ada, g_norm_mix, w_in, g_qnorm, g_knorm, lambda_q1, lambda_k1, lambda_q2,
           lambda_k2, g_ret_norm, g_diff_norm, w_out, g_norm_ffn, w_gate, w_up, w_down):
    depth = w_in.shape[0]
    n_prompt, t, d = x_prompt.shape
    assert n_prompt == 1
    bs = x_sample.shape[0]
    n_pool = cache_k.shape[1]

    log_gamma = jnp.log1p(-jnp.exp2(-5.0 - jnp.arange(N_HEADS, dtype=F32)))
    slopes = jnp.exp2(-8.0 * (jnp.arange(N_HEADS, dtype=F32) + 1.0) / N_HEADS)
    q_scale = D_MAP ** -0.5

    pad = (-(bs + 1)) % 8
    c_all = jnp.concatenate([c_sample, c_prompt, jnp.zeros((pad, d), F32)], axis=0)
    mod = _adaln(c_all, w_ada, b_ada)

    w_in_bf, w_out_bf = w_in.astype(BF16), w_out.astype(BF16)
    wg_bf, wu_bf, wd_bf = w_gate.astype(BF16), w_up.astype(BF16), w_down.astype(BF16)
    cache_k2 = cache_k.reshape(depth, n_pool, PAGE_ROWS, 128)
    cache_v2 = (cache_v.reshape(depth, n_pool, PAGE, N_HEADS, 2, 128).transpose(0, 1, 2, 4, 3, 5)
                .reshape(depth, n_pool, PAGE_ROWS, 128))

    x = x_prompt.reshape(t, d)
    xs = x_sample.reshape(bs, d)
    kv_p = kv_s = s_s = None
    s_p = []
    for l in range(depth):
        lam_init = 0.8 - 0.6 * math.exp(-0.3 * l)
        lam_vecs = [a[l].reshape(1, D_MAP) for a in (lambda_q1, lambda_k1, lambda_q2, lambda_k2)]
        mods_p = [mod[l, bs:bs + 1, i * d:(i + 1) * d] for i in range(N_MOD)]
        mods_s = [mod[l, :bs, i * d:(i + 1) * d] for i in range(N_MOD)]

        obf, g_r, k_leaf, v_leaf, vt = _mixer_in(x, mods_p[0], mods_p[1], g_norm_mix[l], w_in_bf, g_qnorm[l],
                                                 g_knorm[l], l, depth, kv_p, TM_PROMPT, q_scale * LOG2E, with_vt=True)
        kv_p = (k_leaf, v_leaf)
        obf_s, g_s, k_leaf, v_leaf = _mixer_in(xs, mods_s[0], mods_s[1], g_norm_mix[l], w_in_bf, g_qnorm[l],
                                               g_knorm[l], l, depth, kv_s, bs, q_scale)
        kv_s = (k_leaf, v_leaf)

        r, s_fin = _ret_prompt(log_gamma, obf, g_r, g_ret_norm[l], RET_CHUNK)
        s_p.append(s_fin.reshape(1, N_HEADS, D_HEAD, D_HEAD))

        def rows(c):
            return obf_s[:, c * GROUP:(c + 1) * GROUP].astype(F32).reshape(bs, N_HEADS, 1, D_HEAD)

        s_s, r_s = _ret_sample(log_gamma, state_ret, l, rows(0), rows(1), rows(2),
                               g_s.reshape(bs, N_HEADS, 1, D_HEAD), g_ret_norm[l], s_s)
        q_maps = obf_s[:, 3 * GROUP:4 * GROUP].astype(F32).reshape(bs, ROWS, D_MAP)
        dd, dd_s = _attention(slopes, obf, vt, lam_vecs, g_diff_norm[l], lam_init, page_table, q_maps,
                              k_leaf[l].reshape(bs, ROWS, 128), v_leaf[l].reshape(bs, ROWS, 128),
                              cache_k2, cache_v2, l, FLASH_TQ, FLASH_TK, FLASH_SUB, DECODE_PAGES)
        dd_s = dd_s.reshape(bs, 2, N_HEADS, 128).transpose(0, 2, 1, 3).reshape(bs, GROUP)

        x = _dense_tail(r, dd, x, mods_p, l, w_out_bf, g_norm_ffn[l], wg_bf, wu_bf, wd_bf, TM_PROMPT, TF)
        xs = _dense_tail(r_s.reshape(bs, GROUP).astype(BF16), dd_s.astype(BF16), xs, mods_s,
                         l, w_out_bf, g_norm_ffn[l], wg_bf, wu_bf, wd_bf, bs, TF)

    def k_out(a, n):
        return a.reshape(depth, n, -1, N_HEADS, 2, D_MAP)

    def v_out(a, n):
        return (a.reshape(depth, n, -1, 2, N_HEADS, 128).transpose(0, 1, 2, 4, 3, 5)
                .reshape(depth, n, -1, N_HEADS, D_HEAD))

    return (x.reshape(1, t, d), xs.reshape(bs, 1, d), k_out(kv_p[0], 1), v_out(kv_p[1], 1), jnp.stack(s_p),
            k_out(kv_s[0], bs), v_out(kv_s[1], bs), s_s)
```

```python
import functools
import math

import jax
import jax.numpy as jnp
from jax import lax
from jax.experimental import pallas as pl
from jax.experimental.pallas import tpu as pltpu

F32 = jnp.float32
BF16 = jnp.bfloat16

D_MODEL = 2048
N_HEADS = 4
D_HEAD = 256
D_MAP = 128
GROUP = 1024
N_GROUPS = 7
D_FF = 5632
N_MOD = 6
EPS = 1e-6
PAGE = 128
ROWS = 2 * N_HEADS
VMEM_LIMIT = 56 * 1024 * 1024
LOG2E = math.log2(math.e)

TM_PROMPT = 512
TM_IN = 256
TF = 512
RET_CHUNK = 256
FLASH_TQ = 1024
FLASH_TK = 512
FLASH_SUB = 2
DECODE_PAGES = 16

NT_DIMS = (((1,), (1,)), ((), ()))
TN_DIMS = (((0,), (0,)), ((), ()))


def _silu(x):
    return x / (1.0 + jnp.exp(-x))


def _rms(x, g):
    return x * lax.rsqrt(jnp.mean(x * x, axis=-1, keepdims=True) + EPS) * g


def _params(*sem):
    return pltpu.CompilerParams(dimension_semantics=sem, vmem_limit_bytes=VMEM_LIMIT)


def _adaln_kernel(c_ref, w_ref, b_ref, o_ref):
    a = _silu(c_ref[...]).astype(BF16)
    o_ref[0] = jnp.dot(a, w_ref[0].astype(BF16), preferred_element_type=F32) + b_ref[0]


def _adaln(c_all, w_ada, b_ada, tn=1024):
    depth, d, n = w_ada.shape
    rows = c_all.shape[0]
    return pl.pallas_call(
        _adaln_kernel,
        grid=(depth, n // tn),
        in_specs=[pl.BlockSpec((rows, d), lambda l, j: (0, 0)),
                  pl.BlockSpec((1, d, tn), lambda l, j: (l, 0, j)),
                  pl.BlockSpec((1, 1, tn), lambda l, j: (l, 0, j))],
        out_specs=pl.BlockSpec((1, rows, tn), lambda l, j: (l, 0, j)),
        out_shape=jax.ShapeDtypeStruct((depth, rows, n), F32),
        compiler_params=_params("parallel", "parallel"),
        name="adaln",
    )(c_all, w_ada, b_ada.reshape(depth, 1, n))


def _mixer_in_kernel(*refs, q_scale, with_vt, aliased):
    x_ref, sh_ref, sc_ref, gn_ref, w_ref, gq_ref, gk_ref = refs[:7]
    obf_ref, g_ref, k_ref, v_ref = refs[7 + 2 * aliased:11 + 2 * aliased]
    vt_ref = refs[-1] if with_vt else None
    tm = x_ref.shape[0]

    y = _rms(x_ref[...], gn_ref[...])
    h = (y * (1.0 + sc_ref[...]) + sh_ref[...]).astype(BF16)

    def proj(j):
        return jnp.dot(h, w_ref[:, j * GROUP:(j + 1) * GROUP], preferred_element_type=F32)

    def maps(j):
        return [slice(j * GROUP + c * D_MAP, j * GROUP + (c + 1) * D_MAP) for c in range(GROUP // D_MAP)]

    obf_ref[:, 0:GROUP] = proj(0).astype(BF16)
    obf_ref[:, GROUP:2 * GROUP] = (proj(1) * (D_HEAD ** -0.5)).astype(BF16)
    obf_ref[:, 2 * GROUP:3 * GROUP] = proj(2).astype(BF16)
    g_ref[...] = proj(3)

    acc = proj(4)
    for c, dst in enumerate(maps(3)):
        obf_ref[:, dst] = (_rms(acc[:, c * D_MAP:(c + 1) * D_MAP], gq_ref[...]) * q_scale).astype(BF16)

    acc = proj(5)
    for c, dst in enumerate(maps(4)):
        kn = _rms(acc[:, c * D_MAP:(c + 1) * D_MAP], gk_ref[...])
        k_ref[pl.ds(c, tm, stride=ROWS), :] = kn
        obf_ref[:, dst] = kn.astype(BF16)

    acc = proj(6)
    for c in range(GROUP // D_MAP):
        head, half = divmod(c, 2)
        v_ref[pl.ds(half * N_HEADS + head, tm, stride=ROWS), :] = acc[:, c * D_MAP:(c + 1) * D_MAP]
    obf_ref[:, 5 * GROUP:6 * GROUP] = acc.astype(BF16)
    if with_vt:
        vt_ref[...] = acc.T.astype(BF16)


def _mixer_in(x, shift, scale, g_norm, w_bf, g_q, g_k, layer, depth, kv_leaves, tm, q_scale, with_vt=False):
    m, d = x.shape
    aliased = kv_leaves is not None
    mod_rows = shift.shape[0]
    mod_blk = (1, d) if mod_rows == 1 else (tm, d)
    mod_map = (lambda i: (0, 0)) if mod_rows == 1 else (lambda i: (i, 0))
    leaf_spec = pl.BlockSpec((None, tm * ROWS, 128), lambda i: (layer, i, 0))
    leaf_shape = jax.ShapeDtypeStruct((depth, m * ROWS, 128), F32)
    vt_spec = [pl.BlockSpec((GROUP, tm), lambda i: (0, i))] if with_vt else []
    vt_shape = [jax.ShapeDtypeStruct((GROUP, m), BF16)] if with_vt else []
    any_spec = pl.BlockSpec(memory_space=pl.ANY)
    return pl.pallas_call(
        functools.partial(_mixer_in_kernel, q_scale=q_scale, with_vt=with_vt, aliased=aliased),
        grid=(m // tm,),
        in_specs=[pl.BlockSpec((tm, d), lambda i: (i, 0)),
                  pl.BlockSpec(mod_blk, mod_map),
                  pl.BlockSpec(mod_blk, mod_map),
                  pl.BlockSpec((1, d), lambda i: (0, 0)),
                  pl.BlockSpec((None, d, N_GROUPS * GROUP), lambda i: (layer, 0, 0), pipeline_mode=pl.Buffered(1)),
                  pl.BlockSpec((1, D_MAP), lambda i: (0, 0)),
                  pl.BlockSpec((1, D_MAP), lambda i: (0, 0))] + [any_spec, any_spec] * aliased,
        out_specs=[pl.BlockSpec((tm, 6 * GROUP), lambda i: (i, 0)),
                   pl.BlockSpec((tm, GROUP), lambda i: (i, 0)), leaf_spec, leaf_spec] + vt_spec,
        out_shape=[jax.ShapeDtypeStruct((m, 6 * GROUP), BF16),
                   jax.ShapeDtypeStruct((m, GROUP), F32), leaf_shape, leaf_shape] + vt_shape,
        input_output_aliases={7: 2, 8: 3} if aliased else {},
        compiler_params=_params("parallel"),
        name="mixer_in",
    )(x, shift, scale, g_norm.reshape(1, d), w_bf, g_q.reshape(1, D_MAP), g_k.reshape(1, D_MAP),
      *(kv_leaves if aliased else ()))


def _ret_prompt_kernel(lg_ref, q_ref, k_ref, v_ref, g_ref, gn_ref, r_ref, s_ref, dec_scr, *, chunk):
    c = pl.program_id(0)
    row = lax.broadcasted_iota(jnp.int32, (chunk, chunk), 0)
    col = lax.broadcasted_iota(jnp.int32, (chunk, chunk), 1)
    pos = lax.broadcasted_iota(jnp.int32, (chunk, 1), 0).astype(F32)

    @pl.when(c == 0)
    def _():
        s_ref[...] = jnp.zeros_like(s_ref)
        diff = (row - col).astype(F32)
        for h in range(N_HEADS):
            dec_scr[h] = jnp.where(diff >= 0, jnp.exp(jnp.maximum(diff, 0.0) * lg_ref[h]), 0.0)

    for h in range(N_HEADS):
        lg = lg_ref[h]
        sl = slice(h * D_HEAD, (h + 1) * D_HEAD)
        q = q_ref[:, sl]
        k = k_ref[:, sl]
        v = v_ref[:, sl]
        q_decay = jnp.exp((pos + 1.0) * lg)
        k_decay = jnp.exp((chunk - 1.0 - pos) * lg)
        chunk_decay = jnp.exp(jnp.full((1, 1), chunk, F32) * lg)
        state = s_ref[h]

        scores = lax.dot_general(q, k, NT_DIMS, preferred_element_type=F32) * dec_scr[h]
        inner = jnp.dot(scores.astype(BF16), v, preferred_element_type=F32)
        cross = jnp.dot(q, state.astype(BF16), preferred_element_type=F32) * q_decay
        k_dec = (k.astype(F32) * k_decay).astype(BF16)
        s_ref[h] = chunk_decay * state + lax.dot_general(k_dec, v, TN_DIMS, preferred_element_type=F32)

        r_ref[:, sl] = (_rms(inner + cross, gn_ref[h]) * _silu(g_ref[:, sl])).astype(BF16)


def _ret_prompt(log_gamma, obf, g_r, g_ret_norm, chunk):
    t = obf.shape[0]
    return pl.pallas_call(
        functools.partial(_ret_prompt_kernel, chunk=chunk),
        grid=(t // chunk,),
        in_specs=[pl.BlockSpec(memory_space=pltpu.SMEM),
                  pl.BlockSpec((chunk, GROUP), lambda c: (c, 0)),
                  pl.BlockSpec((chunk, GROUP), lambda c: (c, 1)),
                  pl.BlockSpec((chunk, GROUP), lambda c: (c, 2)),
                  pl.BlockSpec((chunk, GROUP), lambda c: (c, 0)),
                  pl.BlockSpec((N_HEADS, 1, D_HEAD), lambda c: (0, 0, 0))],
        out_specs=[pl.BlockSpec((chunk, GROUP), lambda c: (c, 0)),
                   pl.BlockSpec((N_HEADS, D_HEAD, D_HEAD), lambda c: (0, 0, 0))],
        out_shape=[jax.ShapeDtypeStruct((t, GROUP), BF16),
                   jax.ShapeDtypeStruct((N_HEADS, D_HEAD, D_HEAD), F32)],
        scratch_shapes=[pltpu.VMEM((N_HEADS, chunk, chunk), F32)],
        compiler_params=_params("arbitrary"),
        name="ret_prompt",
    )(log_gamma, obf, obf, obf, g_r, g_ret_norm.reshape(N_HEADS, 1, D_HEAD))


def _lambda(lq1_ref, lk1_ref, lq2_ref, lk2_ref, lam_init):
    s1 = jnp.sum(lq1_ref[...] * lk1_ref[...], axis=-1, keepdims=True)
    s2 = jnp.sum(lq2_ref[...] * lk2_ref[...], axis=-1, keepdims=True)
    return jnp.exp(s1) - jnp.exp(s2) + lam_init


PAGE_ROWS = PAGE * ROWS
_V_ROW_SRC = tuple(2 * (j % N_HEADS) for j in range(ROWS))


def _attention_kernel(qi_tab, ki_tab, kpg_tab, vpg_tab, slope_ref, q_ref, k_ref, vt_ref, lq1_ref, lk1_ref, lq2_ref,
                      lk2_ref, gd_ref, qs_ref, kn_ref, vn_ref, slope_rows_ref, diag_ref, gd_rows_ref, *rest,
                      tq, tk, n_sub, lam_init, n_pages, past_len, n_seq):
    k_pages = rest[:n_pages]
    v_pages = rest[n_pages:2 * n_pages]
    o_ref, od_ref, m_scr, l_scr, acc_scr, logit_scr, w_scr, dacc_scr = rest[2 * n_pages:]
    h = pl.program_id(0)
    t = pl.program_id(1)

    n_chunks = past_len // (n_pages * PAGE)
    g = h * pl.num_programs(1) + t
    phase = g % n_chunks
    seq_k = g // n_chunks
    seq_v = seq_k - 1
    k_live = seq_k < n_seq
    v_live = jnp.logical_and(seq_v >= 0, seq_v < n_seq)
    lane = lax.broadcasted_iota(jnp.int32, (ROWS, 128), 1)

    def lane_sum(x):
        return jnp.dot(x.astype(BF16), jnp.ones((128, 128), BF16), preferred_element_type=F32)

    @pl.when(g == 0)
    def _():
        w_scr[...] = jnp.zeros_like(w_scr)
        dacc_scr[...] = jnp.zeros_like(dacc_scr)

    @pl.when(jnp.logical_and(v_live, phase == 0))
    def _():
        logits = logit_scr[...]
        e = jnp.exp(logits - jnp.max(logits, axis=-1, keepdims=True))
        a = e / jnp.sum(e, axis=-1, keepdims=True)
        lam = _lambda(lq1_ref, lk1_ref, lq2_ref, lk2_ref, lam_init)
        dd = a - lam * pltpu.roll(a, ROWS - 1, 0)
        row = lax.broadcasted_iota(jnp.int32, a.shape, 0)
        w = jnp.zeros_like(a)
        for j in range(ROWS):
            w = jnp.where(row == j, pltpu.roll(dd, (j - _V_ROW_SRC[j]) % ROWS, 0), w)
        w_scr[...] = w
        dacc_scr[...] = jnp.zeros_like(dacc_scr)

    @pl.when(jnp.logical_and(k_live, phase == 0))
    def _():
        l_new = jnp.sum(qs_ref[0] * kn_ref[0], axis=-1, keepdims=True)
        logit_scr[:, past_len:] = jnp.where(lane == 0, l_new, -jnp.inf)

    def score_k_chunk():
        q = qs_ref[0]
        diag = diag_ref[...]
        for i in range(n_pages):
            tok0 = pl.multiple_of((phase * n_pages + i) * PAGE, PAGE)
            prod = (k_pages[i][...].reshape(PAGE, ROWS, 128) * q[None]).reshape(PAGE_ROWS, 128)
            sums = lane_sum(prod).reshape(PAGE, ROWS, 128)
            logits = jnp.sum(sums * diag, axis=0)
            dist = (past_len - tok0 - lane).astype(F32)
            logit_scr[:, pl.ds(tok0, PAGE)] = logits - slope_rows_ref[...] * dist

    def apply_v_chunk():
        diag = diag_ref[...]
        acc = dacc_scr[...]
        for i in range(n_pages):
            tok0 = pl.multiple_of((phase * n_pages + i) * PAGE, PAGE)
            spread = (w_scr[:, pl.ds(tok0, PAGE)][None] * diag).reshape(PAGE_ROWS, 128)
            w_rep = lane_sum(spread).reshape(PAGE, ROWS, 128)
            acc = acc + jnp.sum(w_rep * v_pages[i][...].reshape(PAGE, ROWS, 128), axis=0)
        dacc_scr[...] = acc

    qi = qi_tab[t]
    ki = ki_tab[t]
    tkb = tk * n_sub
    ratio = tq // tkb
    slope = slope_ref[h] * LOG2E

    @pl.when(ki == 0)
    def _():
        m_scr[...] = jnp.full_like(m_scr, -jnp.inf)
        l_scr[...] = jnp.zeros_like(l_scr)
        acc_scr[...] = jnp.zeros_like(acc_scr)

    def tile(masked, sub):
        q = q_ref[...]
        k = k_ref[sub * tk:(sub + 1) * tk, :]
        vt = vt_ref[:, sub * tk:(sub + 1) * tk]
        koff = ki * tkb + sub * tk - qi * tq
        krow = lax.broadcasted_iota(jnp.int32, (tk, 128), 0)
        bias = pltpu.repeat(slope * (krow + koff).astype(F32), tq // 128, axis=1)
        if masked:
            kpos = lax.broadcasted_iota(jnp.int32, (tk, tq), 0) + koff
            keep = kpos <= lax.broadcasted_iota(jnp.int32, (tk, tq), 1)
        for m in range(2):
            sl = slice(m * D_MAP, (m + 1) * D_MAP)
            s = lax.dot_general(k[:, sl], q[:, sl], NT_DIMS, preferred_element_type=F32) + bias
            if masked:
                s = jnp.where(keep, s, -jnp.inf)
            m_old = m_scr[m]
            m_new = jnp.maximum(m_old, jnp.max(s, axis=0, keepdims=True))
            alpha = jnp.exp2(m_old - m_new)
            p = jnp.exp2(s - m_new)
            l_scr[m] = alpha * l_scr[m] + jnp.sum(p, axis=0, keepdims=True)
            acc_scr[m] = alpha * acc_scr[m] + jnp.dot(vt, p.astype(BF16), preferred_element_type=F32)
            m_scr[m] = m_new

    def step(masked):
        tile(masked, 0)
        score_k_chunk()
        for sub in range(1, n_sub):
            tile(masked, sub)
        apply_v_chunk()

    @pl.when(ki < qi * ratio)
    def _():
        step(False)

    @pl.when(ki >= qi * ratio)
    def _():
        step(True)

    @pl.when(ki == (qi + 1) * ratio - 1)
    def _():
        lam = _lambda(lq1_ref, lk1_ref, lq2_ref, lk2_ref, lam_init)
        o = acc_scr[0] * (1.0 / l_scr[0]) - lam * (acc_scr[1] * (1.0 / l_scr[1]))
        y = o * lax.rsqrt(jnp.mean(o * o, axis=0, keepdims=True) + EPS)
        y = y * pltpu.repeat(gd_ref[...], tq // 128, axis=1) * (1.0 - lam_init)
        o_ref[...] = y.T.astype(BF16)

    @pl.when(jnp.logical_and(v_live, phase == n_chunks - 1))
    def _():
        out = dacc_scr[...] + w_scr[:, past_len:][:, :1] * vn_ref[0]
        sq = jnp.sum(out * out, axis=-1, keepdims=True)
        ms = (sq + pltpu.roll(sq, N_HEADS, 0)) / D_HEAD
        od_ref[0] = out * lax.rsqrt(ms + EPS) * gd_rows_ref[...] * (1.0 - lam_init)


def _attention(slopes, obf, vt, lam_vecs, g_diff_norm, lam_init, page_table, q_maps, k_new, v_new,
               cache_k, cache_v, layer, tq, tk, n_sub, n_pages):
    t = obf.shape[0]
    tkb = tk * n_sub
    ratio = tq // tkb
    pairs = [(qi, ki) for qi in range(t // tq) for ki in range((qi + 1) * ratio)]
    n_steps = len(pairs)
    qi_tab = jnp.asarray([p[0] for p in pairs], jnp.int32)
    ki_tab = jnp.asarray([p[1] for p in pairs], jnp.int32)
    q_blk, k_blk = 3 * N_HEADS, 4 * N_HEADS
    gd_rep = jnp.broadcast_to(g_diff_norm.reshape(D_HEAD, 1), (D_HEAD, 128))

    n_seq, pages_per_seq = page_table.shape
    past_len = pages_per_seq * PAGE
    n_chunks = pages_per_seq // n_pages
    last = n_seq * n_chunks - 1
    assert N_HEADS * n_steps >= (n_seq + 1) * n_chunks, "not enough grid steps to stream the cache"
    slope_rows = jnp.broadcast_to(jnp.repeat(slopes, 2)[:, None], (ROWS, 128))
    diag = jnp.broadcast_to(jnp.eye(PAGE, 128, dtype=F32)[:, None, :], (PAGE, ROWS, 128))
    gd_rows = jnp.repeat(g_diff_norm.reshape(2, 128), N_HEADS, axis=0)

    def step(h, t):
        return h * n_steps + t

    g_all = jnp.arange(N_HEADS * n_steps)
    chunk_pages = page_table.reshape(n_seq * n_chunks, n_pages)
    kpg_tab = chunk_pages[jnp.minimum(g_all, last)].reshape(-1)
    vpg_tab = chunk_pages[jnp.clip(g_all - n_chunks, 0, last)].reshape(-1)

    def k_spec(i):
        return pl.BlockSpec((None, None, PAGE_ROWS, 128),
                            lambda h, t, qt, kt, kp, vp: (layer, kp[step(h, t) * n_pages + i], 0, 0))

    def v_spec(i):
        return pl.BlockSpec((None, None, PAGE_ROWS, 128),
                            lambda h, t, qt, kt, kp, vp: (layer, vp[step(h, t) * n_pages + i], 0, 0))

    vec = pl.BlockSpec((1, D_MAP), lambda h, t, qt, kt, kp, vp: (0, 0))
    tok_k = pl.BlockSpec((1, ROWS, 128),
                         lambda h, t, qt, kt, kp, vp: (jnp.minimum(step(h, t) // n_chunks, n_seq - 1), 0, 0))
    tok_v = pl.BlockSpec((1, ROWS, 128),
                         lambda h, t, qt, kt, kp, vp: (jnp.clip(step(h, t) // n_chunks - 1, 0, n_seq - 1), 0, 0))
    const = pl.BlockSpec((ROWS, 128), lambda h, t, qt, kt, kp, vp: (0, 0))
    grid_spec = pltpu.PrefetchScalarGridSpec(
        num_scalar_prefetch=4,
        grid=(N_HEADS, n_steps),
        in_specs=[pl.BlockSpec(memory_space=pltpu.SMEM),
                  pl.BlockSpec((tq, D_HEAD), lambda h, t, qt, kt, kp, vp: (qt[t], q_blk + h)),
                  pl.BlockSpec((tkb, D_HEAD), lambda h, t, qt, kt, kp, vp: (kt[t], k_blk + h)),
                  pl.BlockSpec((D_HEAD, tkb), lambda h, t, qt, kt, kp, vp: (h, kt[t])),
                  vec, vec, vec, vec,
                  pl.BlockSpec((D_HEAD, 128), lambda h, t, qt, kt, kp, vp: (0, 0)),
                  tok_k, tok_k, tok_v, const,
                  pl.BlockSpec((PAGE, ROWS, 128), lambda h, t, qt, kt, kp, vp: (0, 0, 0)),
                  const]
                 + [k_spec(i) for i in range(n_pages)] + [v_spec(i) for i in range(n_pages)],
        out_specs=[pl.BlockSpec((tq, D_HEAD), lambda h, t, qt, kt, kp, vp: (qt[t], h)), tok_v],
        scratch_shapes=[pltpu.VMEM((2, 1, tq), F32),
                        pltpu.VMEM((2, 1, tq), F32),
                        pltpu.VMEM((2, D_HEAD, tq), F32),
                        pltpu.VMEM((ROWS, past_len + 128), F32),
                        pltpu.VMEM((ROWS, past_len + 128), F32),
                        pltpu.VMEM((ROWS, 128), F32)],
    )
    return pl.pallas_call(
        functools.partial(_attention_kernel, tq=tq, tk=tk, n_sub=n_sub, lam_init=lam_init, n_pages=n_pages,
                          past_len=past_len, n_seq=n_seq),
        grid_spec=grid_spec,
        out_shape=[jax.ShapeDtypeStruct((t, GROUP), BF16), jax.ShapeDtypeStruct((n_seq, ROWS, 128), F32)],
        compiler_params=_params("arbitrary", "arbitrary"),
        name="attention",
    )(qi_tab, ki_tab, kpg_tab, vpg_tab, slopes, obf, obf, vt, *lam_vecs, gd_rep, q_maps, k_new, v_new, slope_rows,
      diag, gd_rows, *([cache_k] * n_pages), *([cache_v] * n_pages))


def _mixer_out_kernel(r_ref, d_ref, wr_ref, wd_ref, x_ref, gt_ref, sh_ref, sc_ref, gn_ref, xo_ref, h_ref):
    o = (jnp.dot(r_ref[...], wr_ref[...], preferred_element_type=F32)
         + jnp.dot(d_ref[...], wd_ref[...], preferred_element_type=F32))
    x = x_ref[...] + gt_ref[...] * o
    xo_ref[...] = x
    h_ref[...] = (_rms(x, gn_ref[...]) * (1.0 + sc_ref[...]) + sh_ref[...]).astype(BF16)


def _mixer_out(r, dd, w_out_bf, layer, x, gate, shift, scale, g_norm, tm):
    m, d = x.shape
    mod_rows = gate.shape[0]
    mod = pl.BlockSpec((1, d), lambda i: (0, 0)) if mod_rows == 1 else pl.BlockSpec((tm, d), lambda i: (i, 0))
    return pl.pallas_call(
        _mixer_out_kernel,
        grid=(m // tm,),
        in_specs=[pl.BlockSpec((tm, GROUP), lambda i: (i, 0)),
                  pl.BlockSpec((tm, GROUP), lambda i: (i, 0)),
                  pl.BlockSpec((None, GROUP, d), lambda i: (layer, 0, 0)),
                  pl.BlockSpec((None, GROUP, d), lambda i: (layer, 1, 0)),
                  pl.BlockSpec((tm, d), lambda i: (i, 0)),
                  mod, mod, mod,
                  pl.BlockSpec((1, d), lambda i: (0, 0))],
        out_specs=[pl.BlockSpec((tm, d), lambda i: (i, 0)),
                   pl.BlockSpec((tm, d), lambda i: (i, 0))],
        out_shape=[jax.ShapeDtypeStruct((m, d), F32), jax.ShapeDtypeStruct((m, d), BF16)],
        compiler_params=_params("parallel"),
        name="mixer_out",
    )(r, dd, w_out_bf, w_out_bf, x, gate, shift, scale, g_norm.reshape(1, d))


def _ffn_kernel(h_ref, wg_ref, wu_ref, wd_ref, x_ref, gt_ref, o_ref, acc_scr):
    f = pl.program_id(1)

    @pl.when(f == 0)
    def _():
        acc_scr[...] = jnp.zeros_like(acc_scr)

    h = h_ref[...]
    half = wg_ref.shape[1] // 2
    part = None
    for c in range(2):
        sl = slice(c * half, (c + 1) * half)
        a = (_silu(jnp.dot(h, wg_ref[:, sl], preferred_element_type=F32))
             * jnp.dot(h, wu_ref[:, sl], preferred_element_type=F32)).astype(BF16)
        p = jnp.dot(a, wd_ref[sl, :], preferred_element_type=F32)
        part = p if part is None else part + p
    acc_scr[...] += part

    @pl.when(f == pl.num_programs(1) - 1)
    def _():
        o_ref[...] = x_ref[...] + gt_ref[...] * acc_scr[...]


def _ffn(h, wg_bf, wu_bf, wd_bf, layer, x, gate, tm, tf):
    m, d = x.shape
    ff = wg_bf.shape[2]
    mod_rows = gate.shape[0]
    mod = (pl.BlockSpec((1, d), lambda i, f: (0, 0)) if mod_rows == 1
           else pl.BlockSpec((tm, d), lambda i, f: (i, 0)))
    return pl.pallas_call(
        _ffn_kernel,
        grid=(m // tm, ff // tf),
        in_specs=[pl.BlockSpec((tm, d), lambda i, f: (i, 0)),
                  pl.BlockSpec((None, d, tf), lambda i, f: (layer, 0, f)),
                  pl.BlockSpec((None, d, tf), lambda i, f: (layer, 0, f)),
                  pl.BlockSpec((None, tf, d), lambda i, f: (layer, f, 0)),
                  pl.BlockSpec((tm, d), lambda i, f: (i, 0)),
                  mod],
        out_specs=pl.BlockSpec((tm, d), lambda i, f: (i, 0)),
        out_shape=jax.ShapeDtypeStruct((m, d), F32),
        scratch_shapes=[pltpu.VMEM((tm, d), F32)],
        compiler_params=_params("parallel", "arbitrary"),
        name="ffn",
    )(h, wg_bf, wu_bf, wd_bf, x, gate)


def _ret_sample_kernel(*refs, aliased):
    lg_ref, s_ref, q_ref, k_ref, v_ref, g_ref, gn_ref = refs[:7]
    so_ref, r_ref = refs[7 + aliased:]
    ones = jnp.ones((D_HEAD, 128), BF16)
    eye = (lax.broadcasted_iota(jnp.int32, (D_HEAD, D_HEAD), 0)
           == lax.broadcasted_iota(jnp.int32, (D_HEAD, D_HEAD), 1))

    def column(row):
        rep = jnp.dot(jnp.where(eye, row, 0.0).astype(BF16), ones, preferred_element_type=F32)
        return pltpu.repeat(rep, D_HEAD // 128, axis=1)

    for h in range(N_HEADS):
        gamma = jnp.exp(jnp.full((1, 1), lg_ref[h], F32))
        state = s_ref[0, h]
        q = q_ref[0, h]
        k = k_ref[0, h]
        vr = v_ref[0, h]
        so_ref[0, h] = gamma * state + column(k) * vr
        qk = jnp.sum(q * k, axis=-1, keepdims=True)
        o = qk * vr + gamma * jnp.sum(column(q) * state, axis=0, keepdims=True)
        r_ref[0, h] = _rms(o, gn_ref[h]) * _silu(g_ref[0, h])


def _ret_sample(log_gamma, state_all, layer, q_row, k_row, v_row, g_row, g_ret_norm, s_prev):
    depth, b = state_all.shape[:2]
    aliased = s_prev is not None
    row = pl.BlockSpec((1, N_HEADS, 1, D_HEAD), lambda i: (i, 0, 0, 0))
    st = pl.BlockSpec((None, 1, N_HEADS, D_HEAD, D_HEAD), lambda i: (layer, i, 0, 0, 0))
    return pl.pallas_call(
        functools.partial(_ret_sample_kernel, aliased=aliased),
        grid=(b,),
        in_specs=[pl.BlockSpec(memory_space=pltpu.SMEM), st, row, row, row, row,
                  pl.BlockSpec((N_HEADS, 1, D_HEAD), lambda i: (0, 0, 0))]
                 + [pl.BlockSpec(memory_space=pl.ANY)] * aliased,
        out_specs=[st, row],
        out_shape=[jax.ShapeDtypeStruct(state_all.shape, F32),
                   jax.ShapeDtypeStruct((b, N_HEADS, 1, D_HEAD), F32)],
        input_output_aliases={7: 0} if aliased else {},
        compiler_params=_params("parallel"),
        name="ret_sample",
    )(log_gamma, state_all, q_row, k_row, v_row, g_row, g_ret_norm.reshape(N_HEADS, 1, D_HEAD),
      *((s_prev,) if aliased else ()))


def _dense_tail(r, dd, x, mods, layer, w_out_bf, g_norm_ffn, wg_bf, wu_bf, wd_bf, tm, tf):
    _, _, gt_a, sh_f, sc_f, gt_f = mods
    x, h = _mixer_out(r, dd, w_out_bf, layer, x, gt_a, sh_f, sc_f, g_norm_ffn, tm)
    return _ffn(h, wg_bf, wu_bf, wd_bf, layer, x, gt_f, tm, tf)


def kernel(x_prompt, x_sample, cache_k, cache_v, state_ret, page_table, c_prompt, c_sample,
           w_ada, b_ada, g_norm_mix, w_in, g_qnorm, g_knorm, lambda_q1, lambda_k1, lambda_q2,
           lambda_k2, g_ret_norm, g_diff_norm, w_out, g_norm_ffn, w_gate, w_up, w_down):
    depth = w_in.shape[0]
    n_prompt, t, d = x_prompt.shape
    assert n_prompt == 1
    bs = x_sample.shape[0]
    n_pool = cache_k.shape[1]

    log_gamma = jnp.log1p(-jnp.exp2(-5.0 - jnp.arange(N_HEADS, dtype=F32)))
    slopes = jnp.exp2(-8.0 * (jnp.arange(N_HEADS, dtype=F32) + 1.0) / N_HEADS)
    q_scale = D_MAP ** -0.5

    pad = (-(bs + 1)) % 8
    c_all = jnp.concatenate([c_sample, c_prompt, jnp.zeros((pad, d), F32)], axis=0)
    mod = _adaln(c_all, w_ada, b_ada)

    w_in_bf, w_out_bf = w_in.astype(BF16), w_out.astype(BF16)
    wg_bf, wu_bf, wd_bf = w_gate.astype(BF16), w_up.astype(BF16), w_down.astype(BF16)
    cache_k2 = cache_k.reshape(depth, n_pool, PAGE_ROWS, 128)
    cache_v2 = (cache_v.reshape(depth, n_pool, PAGE, N_HEADS, 2, 128).transpose(0, 1, 2, 4, 3, 5)
                .reshape(depth, n_pool, PAGE_ROWS, 128))

    x = x_prompt.reshape(t, d)
    xs = x_sample.reshape(bs, d)
    kv_p = kv_s = s_s = None
    s_p = []
    for l in range(depth):
        lam_init = 0.8 - 0.6 * math.exp(-0.3 * l)
        lam_vecs = [a[l].reshape(1, D_MAP) for a in (lambda_q1, lambda_k1, lambda_q2, lambda_k2)]
        mods_p = [mod[l, bs:bs + 1, i * d:(i + 1) * d] for i in range(N_MOD)]
        mods_s = [mod[l, :bs, i * d:(i + 1) * d] for i in range(N_MOD)]

        obf, g_r, k_leaf, v_leaf, vt = _mixer_in(x, mods_p[0], mods_p[1], g_norm_mix[l], w_in_bf, g_qnorm[l],
                                                 g_knorm[l], l, depth, kv_p, TM_IN, q_scale * LOG2E, with_vt=True)
        kv_p = (k_leaf, v_leaf)
        obf_s, g_s, k_leaf, v_leaf = _mixer_in(xs, mods_s[0], mods_s[1], g_norm_mix[l], w_in_bf, g_qnorm[l],
                                               g_knorm[l], l, depth, kv_s, bs, q_scale)
        kv_s = (k_leaf, v_leaf)

        r, s_fin = _ret_prompt(log_gamma, obf, g_r, g_ret_norm[l], RET_CHUNK)
        s_p.append(s_fin.reshape(1, N_HEADS, D_HEAD, D_HEAD))

        def rows(c):
            return obf_s[:, c * GROUP:(c + 1) * GROUP].astype(F32).reshape(bs, N_HEADS, 1, D_HEAD)

        s_s, r_s = _ret_sample(log_gamma, state_ret, l, rows(0), rows(1), rows(2),
                               g_s.reshape(bs, N_HEADS, 1, D_HEAD), g_ret_norm[l], s_s)
        q_maps = obf_s[:, 3 * GROUP:4 * GROUP].astype(F32).reshape(bs, ROWS, D_MAP)
        dd, dd_s = _attention(slopes, obf, vt, lam_vecs, g_diff_norm[l], lam_init, page_table, q_maps,
                              k_leaf[l].reshape(bs, ROWS, 128), v_leaf[l].reshape(bs, ROWS, 128),
                              cache_k2, cache_v2, l, FLASH_TQ, FLASH_TK, FLASH_SUB, DECODE_PAGES)
        dd_s = dd_s.reshape(bs, 2, N_HEADS, 128).transpose(0, 2, 1, 3).reshape(bs, GROUP)

        x = _dense_tail(r, dd, x, mods_p, l, w_out_bf, g_norm_ffn[l], wg_bf, wu_bf, wd_bf, TM_PROMPT, TF)
        xs = _dense_tail(r_s.reshape(bs, GROUP).astype(BF16), dd_s.astype(BF16), xs, mods_s,
                         l, w_out_bf, g_norm_ffn[l], wg_bf, wu_bf, wd_bf, bs, TF)

    def k_out(a, n):
        return a.reshape(depth, n, -1, N_HEADS, 2, D_MAP)

    def v_out(a, n):
        return (a.reshape(depth, n, -1, 2, N_HEADS, 128).transpose(0, 1, 2, 4, 3, 5)
                .reshape(depth, n, -1, N_HEADS, D_HEAD))

    return (x.reshape(1, t, d), xs.reshape(bs, 1, d), k_out(kv_p[0], 1), v_out(kv_p[1], 1), jnp.stack(s_p),
            k_out(kv_s[0], bs), v_out(kv_s[1], bs), s_s)
```

```python
import functools
import math

import jax
import jax.numpy as jnp
from jax import lax
from jax.experimental import pallas as pl
from jax.experimental.pallas import tpu as pltpu

F32 = jnp.float32
BF16 = jnp.bfloat16

D_MODEL = 2048
N_HEADS = 4
D_HEAD = 256
D_MAP = 128
GROUP = 1024
N_GROUPS = 7
D_FF = 5632
N_MOD = 6
EPS = 1e-6
PAGE = 128
ROWS = 2 * N_HEADS
VMEM_LIMIT = 56 * 1024 * 1024
LOG2E = math.log2(math.e)

TM_PROMPT = 512
TM_IN = 256
TF = 512
RET_CHUNK = 256
FLASH_TQ = 1024
FLASH_TK = 512
FLASH_SUB = 2
DECODE_PAGES = 16

NT_DIMS = (((1,), (1,)), ((), ()))
TN_DIMS = (((0,), (0,)), ((), ()))


def _silu(x):
    return x / (1.0 + jnp.exp(-x))


def _rms(x, g):
    return x * lax.rsqrt(jnp.mean(x * x, axis=-1, keepdims=True) + EPS) * g


def _params(*sem):
    return pltpu.CompilerParams(dimension_semantics=sem, vmem_limit_bytes=VMEM_LIMIT)


def _adaln_kernel(c_ref, w_ref, b_ref, o_ref):
    a = _silu(c_ref[...]).astype(BF16)
    o_ref[0] = jnp.dot(a, w_ref[0].astype(BF16), preferred_element_type=F32) + b_ref[0]


def _adaln(c_all, w_ada, b_ada, tn=1024):
    depth, d, n = w_ada.shape
    rows = c_all.shape[0]
    return pl.pallas_call(
        _adaln_kernel,
        grid=(depth, n // tn),
        in_specs=[pl.BlockSpec((rows, d), lambda l, j: (0, 0)),
                  pl.BlockSpec((1, d, tn), lambda l, j: (l, 0, j)),
                  pl.BlockSpec((1, 1, tn), lambda l, j: (l, 0, j))],
        out_specs=pl.BlockSpec((1, rows, tn), lambda l, j: (l, 0, j)),
        out_shape=jax.ShapeDtypeStruct((depth, rows, n), F32),
        compiler_params=_params("parallel", "parallel"),
        name="adaln",
    )(c_all, w_ada, b_ada.reshape(depth, 1, n))


def _mixer_in_kernel(*refs, q_scale, with_vt, aliased):
    x_ref, sh_ref, sc_ref, gn_ref, w_ref, gq_ref, gk_ref = refs[:7]
    obf_ref, g_ref, k_ref, v_ref = refs[7 + 2 * aliased:11 + 2 * aliased]
    qt_ref, vt_ref = refs[-2:] if with_vt else (None, None)
    tm = x_ref.shape[0]

    y = _rms(x_ref[...], gn_ref[...])
    h = (y * (1.0 + sc_ref[...]) + sh_ref[...]).astype(BF16)

    def proj(j):
        return jnp.dot(h, w_ref[:, j * GROUP:(j + 1) * GROUP], preferred_element_type=F32)

    def maps(j):
        return [slice(j * GROUP + c * D_MAP, j * GROUP + (c + 1) * D_MAP) for c in range(GROUP // D_MAP)]

    obf_ref[:, 0:GROUP] = proj(0).astype(BF16)
    obf_ref[:, GROUP:2 * GROUP] = (proj(1) * (D_HEAD ** -0.5)).astype(BF16)
    obf_ref[:, 2 * GROUP:3 * GROUP] = proj(2).astype(BF16)
    g_ref[...] = proj(3)

    acc = proj(4)
    qn = [_rms(acc[:, c * D_MAP:(c + 1) * D_MAP], gq_ref[...]) * q_scale for c in range(GROUP // D_MAP)]
    for q_map, dst in zip(qn, maps(3)):
        obf_ref[:, dst] = q_map.astype(BF16)
    if with_vt:
        qt_ref[...] = jnp.concatenate(qn, axis=1).T.astype(BF16)

    acc = proj(5)
    for c, dst in enumerate(maps(4)):
        kn = _rms(acc[:, c * D_MAP:(c + 1) * D_MAP], gk_ref[...])
        k_ref[pl.ds(c, tm, stride=ROWS), :] = kn
        obf_ref[:, dst] = kn.astype(BF16)

    acc = proj(6)
    for c in range(GROUP // D_MAP):
        head, half = divmod(c, 2)
        v_ref[pl.ds(half * N_HEADS + head, tm, stride=ROWS), :] = acc[:, c * D_MAP:(c + 1) * D_MAP]
    obf_ref[:, 5 * GROUP:6 * GROUP] = acc.astype(BF16)
    if with_vt:
        vt_ref[...] = acc.T.astype(BF16)


def _mixer_in(x, shift, scale, g_norm, w_bf, g_q, g_k, layer, depth, kv_leaves, tm, q_scale, with_vt=False):
    m, d = x.shape
    aliased = kv_leaves is not None
    mod_rows = shift.shape[0]
    mod_blk = (1, d) if mod_rows == 1 else (tm, d)
    mod_map = (lambda i: (0, 0)) if mod_rows == 1 else (lambda i: (i, 0))
    leaf_spec = pl.BlockSpec((None, tm * ROWS, 128), lambda i: (layer, i, 0))
    leaf_shape = jax.ShapeDtypeStruct((depth, m * ROWS, 128), F32)
    vt_spec = [pl.BlockSpec((GROUP, tm), lambda i: (0, i))] * 2 if with_vt else []
    vt_shape = [jax.ShapeDtypeStruct((GROUP, m), BF16)] * 2 if with_vt else []
    any_spec = pl.BlockSpec(memory_space=pl.ANY)
    return pl.pallas_call(
        functools.partial(_mixer_in_kernel, q_scale=q_scale, with_vt=with_vt, aliased=aliased),
        grid=(m // tm,),
        in_specs=[pl.BlockSpec((tm, d), lambda i: (i, 0)),
                  pl.BlockSpec(mod_blk, mod_map),
                  pl.BlockSpec(mod_blk, mod_map),
                  pl.BlockSpec((1, d), lambda i: (0, 0)),
                  pl.BlockSpec((None, d, N_GROUPS * GROUP), lambda i: (layer, 0, 0), pipeline_mode=pl.Buffered(1)),
                  pl.BlockSpec((1, D_MAP), lambda i: (0, 0)),
                  pl.BlockSpec((1, D_MAP), lambda i: (0, 0))] + [any_spec, any_spec] * aliased,
        out_specs=[pl.BlockSpec((tm, 6 * GROUP), lambda i: (i, 0)),
                   pl.BlockSpec((tm, GROUP), lambda i: (i, 0)), leaf_spec, leaf_spec] + vt_spec,
        out_shape=[jax.ShapeDtypeStruct((m, 6 * GROUP), BF16),
                   jax.ShapeDtypeStruct((m, GROUP), F32), leaf_shape, leaf_shape] + vt_shape,
        input_output_aliases={7: 2, 8: 3} if aliased else {},
        compiler_params=_params("parallel"),
        name="mixer_in",
    )(x, shift, scale, g_norm.reshape(1, d), w_bf, g_q.reshape(1, D_MAP), g_k.reshape(1, D_MAP),
      *(kv_leaves if aliased else ()))


def _ret_prompt_kernel(lg_ref, q_ref, k_ref, v_ref, g_ref, gn_ref, r_ref, s_ref, dec_scr, *, chunk):
    c = pl.program_id(0)
    row = lax.broadcasted_iota(jnp.int32, (chunk, chunk), 0)
    col = lax.broadcasted_iota(jnp.int32, (chunk, chunk), 1)
    pos = lax.broadcasted_iota(jnp.int32, (chunk, 1), 0).astype(F32)

    @pl.when(c == 0)
    def _():
        s_ref[...] = jnp.zeros_like(s_ref)
        diff = (row - col).astype(F32)
        for h in range(N_HEADS):
            dec_scr[h] = jnp.where(diff >= 0, jnp.exp(jnp.maximum(diff, 0.0) * lg_ref[h]), 0.0)

    for h in range(N_HEADS):
        lg = lg_ref[h]
        sl = slice(h * D_HEAD, (h + 1) * D_HEAD)
        q = q_ref[:, sl]
        k = k_ref[:, sl]
        v = v_ref[:, sl]
        q_decay = jnp.exp((pos + 1.0) * lg)
        k_decay = jnp.exp((chunk - 1.0 - pos) * lg)
        chunk_decay = jnp.exp(jnp.full((1, 1), chunk, F32) * lg)
        state = s_ref[h]

        scores = lax.dot_general(q, k, NT_DIMS, preferred_element_type=F32) * dec_scr[h]
        inner = jnp.dot(scores.astype(BF16), v, preferred_element_type=F32)
        cross = jnp.dot(q, state.astype(BF16), preferred_element_type=F32) * q_decay
        k_dec = (k.astype(F32) * k_decay).astype(BF16)
        s_ref[h] = chunk_decay * state + lax.dot_general(k_dec, v, TN_DIMS, preferred_element_type=F32)

        r_ref[:, sl] = (_rms(inner + cross, gn_ref[h]) * _silu(g_ref[:, sl])).astype(BF16)


def _ret_prompt(log_gamma, obf, g_r, g_ret_norm, chunk):
    t = obf.shape[0]
    return pl.pallas_call(
        functools.partial(_ret_prompt_kernel, chunk=chunk),
        grid=(t // chunk,),
        in_specs=[pl.BlockSpec(memory_space=pltpu.SMEM),
                  pl.BlockSpec((chunk, GROUP), lambda c: (c, 0)),
                  pl.BlockSpec((chunk, GROUP), lambda c: (c, 1)),
                  pl.BlockSpec((chunk, GROUP), lambda c: (c, 2)),
                  pl.BlockSpec((chunk, GROUP), lambda c: (c, 0)),
                  pl.BlockSpec((N_HEADS, 1, D_HEAD), lambda c: (0, 0, 0))],
        out_specs=[pl.BlockSpec((chunk, GROUP), lambda c: (c, 0)),
                   pl.BlockSpec((N_HEADS, D_HEAD, D_HEAD), lambda c: (0, 0, 0))],
        out_shape=[jax.ShapeDtypeStruct((t, GROUP), BF16),
                   jax.ShapeDtypeStruct((N_HEADS, D_HEAD, D_HEAD), F32)],
        scratch_shapes=[pltpu.VMEM((N_HEADS, chunk, chunk), F32)],
        compiler_params=_params("arbitrary"),
        name="ret_prompt",
    )(log_gamma, obf, obf, obf, g_r, g_ret_norm.reshape(N_HEADS, 1, D_HEAD))


def _lambda(lq1_ref, lk1_ref, lq2_ref, lk2_ref, lam_init):
    s1 = jnp.sum(lq1_ref[...] * lk1_ref[...], axis=-1, keepdims=True)
    s2 = jnp.sum(lq2_ref[...] * lk2_ref[...], axis=-1, keepdims=True)
    return jnp.exp(s1) - jnp.exp(s2) + lam_init


PAGE_ROWS = PAGE * ROWS
_V_ROW_SRC = tuple(2 * (j % N_HEADS) for j in range(ROWS))


def _attention_kernel(qi_tab, ki_tab, kpg_tab, vpg_tab, slope_ref, qt_ref, k_ref, vt_ref, lq1_ref, lk1_ref, lq2_ref,
                      lk2_ref, gd_ref, qs_ref, kn_ref, vn_ref, slope_rows_ref, diag_ref, gd_rows_ref, *rest,
                      tq, tk, n_sub, lam_init, n_pages, past_len, n_seq):
    k_pages = rest[:n_pages]
    v_pages = rest[n_pages:2 * n_pages]
    o_ref, od_ref, m_scr, l_scr, acc_scr, logit_scr, w_scr, dacc_scr = rest[2 * n_pages:]
    h = pl.program_id(0)
    t = pl.program_id(1)

    n_chunks = past_len // (n_pages * PAGE)
    g = h * pl.num_programs(1) + t
    phase = g % n_chunks
    seq_k = g // n_chunks
    seq_v = seq_k - 1
    k_live = seq_k < n_seq
    v_live = jnp.logical_and(seq_v >= 0, seq_v < n_seq)
    lane = lax.broadcasted_iota(jnp.int32, (ROWS, 128), 1)

    def lane_sum(x):
        return jnp.dot(x.astype(BF16), jnp.ones((128, 128), BF16), preferred_element_type=F32)

    @pl.when(g == 0)
    def _():
        w_scr[...] = jnp.zeros_like(w_scr)
        dacc_scr[...] = jnp.zeros_like(dacc_scr)

    @pl.when(jnp.logical_and(v_live, phase == 0))
    def _():
        logits = logit_scr[...]
        e = jnp.exp(logits - jnp.max(logits, axis=-1, keepdims=True))
        a = e / jnp.sum(e, axis=-1, keepdims=True)
        lam = _lambda(lq1_ref, lk1_ref, lq2_ref, lk2_ref, lam_init)
        dd = a - lam * pltpu.roll(a, ROWS - 1, 0)
        row = lax.broadcasted_iota(jnp.int32, a.shape, 0)
        w = jnp.zeros_like(a)
        for j in range(ROWS):
            w = jnp.where(row == j, pltpu.roll(dd, (j - _V_ROW_SRC[j]) % ROWS, 0), w)
        w_scr[...] = w
        dacc_scr[...] = jnp.zeros_like(dacc_scr)

    @pl.when(jnp.logical_and(k_live, phase == 0))
    def _():
        l_new = jnp.sum(qs_ref[0] * kn_ref[0], axis=-1, keepdims=True)
        logit_scr[:, past_len:] = jnp.where(lane == 0, l_new, -jnp.inf)

    def score_k_chunk():
        q = qs_ref[0]
        diag = diag_ref[...]
        for i in range(n_pages):
            tok0 = pl.multiple_of((phase * n_pages + i) * PAGE, PAGE)
            prod = (k_pages[i][...].reshape(PAGE, ROWS, 128) * q[None]).reshape(PAGE_ROWS, 128)
            sums = lane_sum(prod).reshape(PAGE, ROWS, 128)
            logits = jnp.sum(sums * diag, axis=0)
            dist = (past_len - tok0 - lane).astype(F32)
            logit_scr[:, pl.ds(tok0, PAGE)] = logits - slope_rows_ref[...] * dist

    def apply_v_chunk():
        diag = diag_ref[...]
        acc = dacc_scr[...]
        for i in range(n_pages):
            tok0 = pl.multiple_of((phase * n_pages + i) * PAGE, PAGE)
            spread = (w_scr[:, pl.ds(tok0, PAGE)][None] * diag).reshape(PAGE_ROWS, 128)
            w_rep = lane_sum(spread).reshape(PAGE, ROWS, 128)
            acc = acc + jnp.sum(w_rep * v_pages[i][...].reshape(PAGE, ROWS, 128), axis=0)
        dacc_scr[...] = acc

    qi = qi_tab[t]
    ki = ki_tab[t]
    tkb = tk * n_sub
    ratio = tq // tkb
    slope = slope_ref[h] * LOG2E

    @pl.when(ki == 0)
    def _():
        m_scr[...] = jnp.full_like(m_scr, -jnp.inf)
        l_scr[...] = jnp.zeros_like(l_scr)
        acc_scr[...] = jnp.zeros_like(acc_scr)

    def tile(masked, sub):
        qt = qt_ref[...]
        k = k_ref[sub * tk:(sub + 1) * tk, :]
        vt = vt_ref[:, sub * tk:(sub + 1) * tk]
        koff = ki * tkb + sub * tk - qi * tq
        krow = lax.broadcasted_iota(jnp.int32, (tk, 128), 0)
        bias = pltpu.repeat(slope * (krow + koff).astype(F32), tq // 128, axis=1)
        if masked:
            kpos = lax.broadcasted_iota(jnp.int32, (tk, tq), 0) + koff
            keep = kpos <= lax.broadcasted_iota(jnp.int32, (tk, tq), 1)
        for m in range(2):
            sl = slice(m * D_MAP, (m + 1) * D_MAP)
            s = jnp.dot(k[:, sl], qt[sl, :], preferred_element_type=F32) + bias
            if masked:
                s = jnp.where(keep, s, -jnp.inf)
            m_old = m_scr[m]
            m_new = jnp.maximum(m_old, jnp.max(s, axis=0, keepdims=True))
            alpha = jnp.exp2(m_old - m_new)
            p = jnp.exp2(s - m_new)
            l_scr[m] = alpha * l_scr[m] + jnp.sum(p, axis=0, keepdims=True)
            acc_scr[m] = alpha * acc_scr[m] + jnp.dot(vt, p.astype(BF16), preferred_element_type=F32)
            m_scr[m] = m_new

    def step(masked):
        tile(masked, 0)
        score_k_chunk()
        for sub in range(1, n_sub):
            tile(masked, sub)
        apply_v_chunk()

    @pl.when(ki < qi * ratio)
    def _():
        step(False)

    @pl.when(ki >= qi * ratio)
    def _():
        step(True)

    @pl.when(ki == (qi + 1) * ratio - 1)
    def _():
        lam = _lambda(lq1_ref, lk1_ref, lq2_ref, lk2_ref, lam_init)
        o = acc_scr[0] * (1.0 / l_scr[0]) - lam * (acc_scr[1] * (1.0 / l_scr[1]))
        y = o * lax.rsqrt(jnp.mean(o * o, axis=0, keepdims=True) + EPS)
        y = y * pltpu.repeat(gd_ref[...], tq // 128, axis=1) * (1.0 - lam_init)
        o_ref[...] = y.T.astype(BF16)

    @pl.when(jnp.logical_and(v_live, phase == n_chunks - 1))
    def _():
        out = dacc_scr[...] + w_scr[:, past_len:][:, :1] * vn_ref[0]
        sq = jnp.sum(out * out, axis=-1, keepdims=True)
        ms = (sq + pltpu.roll(sq, N_HEADS, 0)) / D_HEAD
        od_ref[0] = out * lax.rsqrt(ms + EPS) * gd_rows_ref[...] * (1.0 - lam_init)


def _attention(slopes, obf, qt, vt, lam_vecs, g_diff_norm, lam_init, page_table, q_maps, k_new, v_new,
               cache_k, cache_v, layer, tq, tk, n_sub, n_pages):
    t = obf.shape[0]
    tkb = tk * n_sub
    ratio = tq // tkb
    pairs = [(qi, ki) for qi in range(t // tq) for ki in range((qi + 1) * ratio)]
    n_steps = len(pairs)
    qi_tab = jnp.asarray([p[0] for p in pairs], jnp.int32)
    ki_tab = jnp.asarray([p[1] for p in pairs], jnp.int32)
    k_blk = 4 * N_HEADS
    gd_rep = jnp.broadcast_to(g_diff_norm.reshape(D_HEAD, 1), (D_HEAD, 128))

    n_seq, pages_per_seq = page_table.shape
    past_len = pages_per_seq * PAGE
    n_chunks = pages_per_seq // n_pages
    last = n_seq * n_chunks - 1
    assert N_HEADS * n_steps >= (n_seq + 1) * n_chunks, "not enough grid steps to stream the cache"
    slope_rows = jnp.broadcast_to(jnp.repeat(slopes, 2)[:, None], (ROWS, 128))
    diag = jnp.broadcast_to(jnp.eye(PAGE, 128, dtype=F32)[:, None, :], (PAGE, ROWS, 128))
    gd_rows = jnp.repeat(g_diff_norm.reshape(2, 128), N_HEADS, axis=0)

    def step(h, t):
        return h * n_steps + t

    g_all = jnp.arange(N_HEADS * n_steps)
    chunk_pages = page_table.reshape(n_seq * n_chunks, n_pages)
    kpg_tab = chunk_pages[jnp.minimum(g_all, last)].reshape(-1)
    vpg_tab = chunk_pages[jnp.clip(g_all - n_chunks, 0, last)].reshape(-1)

    def k_spec(i):
        return pl.BlockSpec((None, None, PAGE_ROWS, 128),
                            lambda h, t, qt, kt, kp, vp: (layer, kp[step(h, t) * n_pages + i], 0, 0))

    def v_spec(i):
        return pl.BlockSpec((None, None, PAGE_ROWS, 128),
                            lambda h, t, qt, kt, kp, vp: (layer, vp[step(h, t) * n_pages + i], 0, 0))

    vec = pl.BlockSpec((1, D_MAP), lambda h, t, qt, kt, kp, vp: (0, 0))
    tok_k = pl.BlockSpec((1, ROWS, 128),
                         lambda h, t, qt, kt, kp, vp: (jnp.minimum(step(h, t) // n_chunks, n_seq - 1), 0, 0))
    tok_v = pl.BlockSpec((1, ROWS, 128),
                         lambda h, t, qt, kt, kp, vp: (jnp.clip(step(h, t) // n_chunks - 1, 0, n_seq - 1), 0, 0))
    const = pl.BlockSpec((ROWS, 128), lambda h, t, qt, kt, kp, vp: (0, 0))
    grid_spec = pltpu.PrefetchScalarGridSpec(
        num_scalar_prefetch=4,
        grid=(N_HEADS, n_steps),
        in_specs=[pl.BlockSpec(memory_space=pltpu.SMEM),
                  pl.BlockSpec((D_HEAD, tq), lambda h, t, qt, kt, kp, vp: (h, qt[t])),
                  pl.BlockSpec((tkb, D_HEAD), lambda h, t, qt, kt, kp, vp: (kt[t], k_blk + h)),
                  pl.BlockSpec((D_HEAD, tkb), lambda h, t, qt, kt, kp, vp: (h, kt[t])),
                  vec, vec, vec, vec,
                  pl.BlockSpec((D_HEAD, 128), lambda h, t, qt, kt, kp, vp: (0, 0)),
                  tok_k, tok_k, tok_v, const,
                  pl.BlockSpec((PAGE, ROWS, 128), lambda h, t, qt, kt, kp, vp: (0, 0, 0)),
                  const]
                 + [k_spec(i) for i in range(n_pages)] + [v_spec(i) for i in range(n_pages)],
        out_specs=[pl.BlockSpec((tq, D_HEAD), lambda h, t, qt, kt, kp, vp: (qt[t], h)), tok_v],
        scratch_shapes=[pltpu.VMEM((2, 1, tq), F32),
                        pltpu.VMEM((2, 1, tq), F32),
                        pltpu.VMEM((2, D_HEAD, tq), F32),
                        pltpu.VMEM((ROWS, past_len + 128), F32),
                        pltpu.VMEM((ROWS, past_len + 128), F32),
                        pltpu.VMEM((ROWS, 128), F32)],
    )
    return pl.pallas_call(
        functools.partial(_attention_kernel, tq=tq, tk=tk, n_sub=n_sub, lam_init=lam_init, n_pages=n_pages,
                          past_len=past_len, n_seq=n_seq),
        grid_spec=grid_spec,
        out_shape=[jax.ShapeDtypeStruct((t, GROUP), BF16), jax.ShapeDtypeStruct((n_seq, ROWS, 128), F32)],
        compiler_params=_params("arbitrary", "arbitrary"),
        name="attention",
    )(qi_tab, ki_tab, kpg_tab, vpg_tab, slopes, qt, obf, vt, *lam_vecs, gd_rep, q_maps, k_new, v_new, slope_rows,
      diag, gd_rows, *([cache_k] * n_pages), *([cache_v] * n_pages))


def _mixer_out_kernel(r_ref, d_ref, wr_ref, wd_ref, x_ref, gt_ref, sh_ref, sc_ref, gn_ref, xo_ref, h_ref):
    o = (jnp.dot(r_ref[...], wr_ref[...], preferred_element_type=F32)
         + jnp.dot(d_ref[...], wd_ref[...], preferred_element_type=F32))
    x = x_ref[...] + gt_ref[...] * o
    xo_ref[...] = x
    h_ref[...] = (_rms(x, gn_ref[...]) * (1.0 + sc_ref[...]) + sh_ref[...]).astype(BF16)


def _mixer_out(r, dd, w_out_bf, layer, x, gate, shift, scale, g_norm, tm):
    m, d = x.shape
    mod_rows = gate.shape[0]
    mod = pl.BlockSpec((1, d), lambda i: (0, 0)) if mod_rows == 1 else pl.BlockSpec((tm, d), lambda i: (i, 0))
    return pl.pallas_call(
        _mixer_out_kernel,
        grid=(m // tm,),
        in_specs=[pl.BlockSpec((tm, GROUP), lambda i: (i, 0)),
                  pl.BlockSpec((tm, GROUP), lambda i: (i, 0)),
                  pl.BlockSpec((None, GROUP, d), lambda i: (layer, 0, 0)),
                  pl.BlockSpec((None, GROUP, d), lambda i: (layer, 1, 0)),
                  pl.BlockSpec((tm, d), lambda i: (i, 0)),
                  mod, mod, mod,
                  pl.BlockSpec((1, d), lambda i: (0, 0))],
        out_specs=[pl.BlockSpec((tm, d), lambda i: (i, 0)),
                   pl.BlockSpec((tm, d), lambda i: (i, 0))],
        out_shape=[jax.ShapeDtypeStruct((m, d), F32), jax.ShapeDtypeStruct((m, d), BF16)],
        compiler_params=_params("parallel"),
        name="mixer_out",
    )(r, dd, w_out_bf, w_out_bf, x, gate, shift, scale, g_norm.reshape(1, d))


def _ffn_kernel(h_ref, wg_ref, wu_ref, wd_ref, x_ref, gt_ref, o_ref, acc_scr):
    f = pl.program_id(1)

    @pl.when(f == 0)
    def _():
        acc_scr[...] = jnp.zeros_like(acc_scr)

    h = h_ref[...]
    half = wg_ref.shape[1] // 2
    part = None
    for c in range(2):
        sl = slice(c * half, (c + 1) * half)
        a = (_silu(jnp.dot(h, wg_ref[:, sl], preferred_element_type=F32))
             * jnp.dot(h, wu_ref[:, sl], preferred_element_type=F32)).astype(BF16)
        p = jnp.dot(a, wd_ref[sl, :], preferred_element_type=F32)
        part = p if part is None else part + p
    acc_scr[...] += part

    @pl.when(f == pl.num_programs(1) - 1)
    def _():
        o_ref[...] = x_ref[...] + gt_ref[...] * acc_scr[...]


def _ffn(h, wg_bf, wu_bf, wd_bf, layer, x, gate, tm, tf):
    m, d = x.shape
    ff = wg_bf.shape[2]
    mod_rows = gate.shape[0]
    mod = (pl.BlockSpec((1, d), lambda i, f: (0, 0)) if mod_rows == 1
           else pl.BlockSpec((tm, d), lambda i, f: (i, 0)))
    return pl.pallas_call(
        _ffn_kernel,
        grid=(m // tm, ff // tf),
        in_specs=[pl.BlockSpec((tm, d), lambda i, f: (i, 0)),
                  pl.BlockSpec((None, d, tf), lambda i, f: (layer, 0, f)),
                  pl.BlockSpec((None, d, tf), lambda i, f: (layer, 0, f)),
                  pl.BlockSpec((None, tf, d), lambda i, f: (layer, f, 0)),
                  pl.BlockSpec((tm, d), lambda i, f: (i, 0)),
                  mod],
        out_specs=pl.BlockSpec((tm, d), lambda i, f: (i, 0)),
        out_shape=jax.ShapeDtypeStruct((m, d), F32),
        scratch_shapes=[pltpu.VMEM((tm, d), F32)],
        compiler_params=_params("parallel", "arbitrary"),
        name="ffn",
    )(h, wg_bf, wu_bf, wd_bf, x, gate)


def _ret_sample_kernel(*refs, aliased):
    lg_ref, s_ref, q_ref, k_ref, v_ref, g_ref, gn_ref = refs[:7]
    so_ref, r_ref = refs[7 + aliased:]
    ones = jnp.ones((D_HEAD, 128), BF16)
    eye = (lax.broadcasted_iota(jnp.int32, (D_HEAD, D_HEAD), 0)
           == lax.broadcasted_iota(jnp.int32, (D_HEAD, D_HEAD), 1))

    def column(row):
        rep = jnp.dot(jnp.where(eye, row, 0.0).astype(BF16), ones, preferred_element_type=F32)
        return pltpu.repeat(rep, D_HEAD // 128, axis=1)

    for h in range(N_HEADS):
        gamma = jnp.exp(jnp.full((1, 1), lg_ref[h], F32))
        state = s_ref[0, h]
        q = q_ref[0, h]
        k = k_ref[0, h]
        vr = v_ref[0, h]
        so_ref[0, h] = gamma * state + column(k) * vr
        qk = jnp.sum(q * k, axis=-1, keepdims=True)
        o = qk * vr + gamma * jnp.sum(column(q) * state, axis=0, keepdims=True)
        r_ref[0, h] = _rms(o, gn_ref[h]) * _silu(g_ref[0, h])


def _ret_sample(log_gamma, state_all, layer, q_row, k_row, v_row, g_row, g_ret_norm, s_prev):
    depth, b = state_all.shape[:2]
    aliased = s_prev is not None
    row = pl.BlockSpec((1, N_HEADS, 1, D_HEAD), lambda i: (i, 0, 0, 0))
    st = pl.BlockSpec((None, 1, N_HEADS, D_HEAD, D_HEAD), lambda i: (layer, i, 0, 0, 0))
    return pl.pallas_call(
        functools.partial(_ret_sample_kernel, aliased=aliased),
        grid=(b,),
        in_specs=[pl.BlockSpec(memory_space=pltpu.SMEM), st, row, row, row, row,
                  pl.BlockSpec((N_HEADS, 1, D_HEAD), lambda i: (0, 0, 0))]
                 + [pl.BlockSpec(memory_space=pl.ANY)] * aliased,
        out_specs=[st, row],
        out_shape=[jax.ShapeDtypeStruct(state_all.shape, F32),
                   jax.ShapeDtypeStruct((b, N_HEADS, 1, D_HEAD), F32)],
        input_output_aliases={7: 0} if aliased else {},
        compiler_params=_params("parallel"),
        name="ret_sample",
    )(log_gamma, state_all, q_row, k_row, v_row, g_row, g_ret_norm.reshape(N_HEADS, 1, D_HEAD),
      *((s_prev,) if aliased else ()))


def _dense_tail(r, dd, x, mods, layer, w_out_bf, g_norm_ffn, wg_bf, wu_bf, wd_bf, tm, tf):
    _, _, gt_a, sh_f, sc_f, gt_f = mods
    x, h = _mixer_out(r, dd, w_out_bf, layer, x, gt_a, sh_f, sc_f, g_norm_ffn, tm)
    return _ffn(h, wg_bf, wu_bf, wd_bf, layer, x, gt_f, tm, tf)


def kernel(x_prompt, x_sample, cache_k, cache_v, state_ret, page_table, c_prompt, c_sample,
           w_ada, b_ada, g_norm_mix, w_in, g_qnorm, g_knorm, lambda_q1, lambda_k1, lambda_q2,
           lambda_k2, g_ret_norm, g_diff_norm, w_out, g_norm_ffn, w_gate, w_up, w_down):
    depth = w_in.shape[0]
    n_prompt, t, d = x_prompt.shape
    assert n_prompt == 1
    bs = x_sample.shape[0]
    n_pool = cache_k.shape[1]

    log_gamma = jnp.log1p(-jnp.exp2(-5.0 - jnp.arange(N_HEADS, dtype=F32)))
    slopes = jnp.exp2(-8.0 * (jnp.arange(N_HEADS, dtype=F32) + 1.0) / N_HEADS)
    q_scale = D_MAP ** -0.5

    pad = (-(bs + 1)) % 8
    c_all = jnp.concatenate([c_sample, c_prompt, jnp.zeros((pad, d), F32)], axis=0)
    mod = _adaln(c_all, w_ada, b_ada)

    w_in_bf, w_out_bf = w_in.astype(BF16), w_out.astype(BF16)
    wg_bf, wu_bf, wd_bf = w_gate.astype(BF16), w_up.astype(BF16), w_down.astype(BF16)
    cache_k2 = cache_k.reshape(depth, n_pool, PAGE_ROWS, 128)
    cache_v2 = (cache_v.reshape(depth, n_pool, PAGE, N_HEADS, 2, 128).transpose(0, 1, 2, 4, 3, 5)
                .reshape(depth, n_pool, PAGE_ROWS, 128))

    x = x_prompt.reshape(t, d)
    xs = x_sample.reshape(bs, d)
    kv_p = kv_s = s_s = None
    s_p = []
    for l in range(depth):
        lam_init = 0.8 - 0.6 * math.exp(-0.3 * l)
        lam_vecs = [a[l].reshape(1, D_MAP) for a in (lambda_q1, lambda_k1, lambda_q2, lambda_k2)]
        mods_p = [mod[l, bs:bs + 1, i * d:(i + 1) * d] for i in range(N_MOD)]
        mods_s = [mod[l, :bs, i * d:(i + 1) * d] for i in range(N_MOD)]

        obf, g_r, k_leaf, v_leaf, qt, vt = _mixer_in(x, mods_p[0], mods_p[1], g_norm_mix[l], w_in_bf, g_qnorm[l],
                                                 g_knorm[l], l, depth, kv_p, TM_IN, q_scale * LOG2E, with_vt=True)
        kv_p = (k_leaf, v_leaf)
        obf_s, g_s, k_leaf, v_leaf = _mixer_in(xs, mods_s[0], mods_s[1], g_norm_mix[l], w_in_bf, g_qnorm[l],
                                               g_knorm[l], l, depth, kv_s, bs, q_scale)
        kv_s = (k_leaf, v_leaf)

        r, s_fin = _ret_prompt(log_gamma, obf, g_r, g_ret_norm[l], RET_CHUNK)
        s_p.append(s_fin.reshape(1, N_HEADS, D_HEAD, D_HEAD))

        def rows(c):
            return obf_s[:, c * GROUP:(c + 1) * GROUP].astype(F32).reshape(bs, N_HEADS, 1, D_HEAD)

        s_s, r_s = _ret_sample(log_gamma, state_ret, l, rows(0), rows(1), rows(2),
                               g_s.reshape(bs, N_HEADS, 1, D_HEAD), g_ret_norm[l], s_s)
        q_maps = obf_s[:, 3 * GROUP:4 * GROUP].astype(F32).reshape(bs, ROWS, D_MAP)
        dd, dd_s = _attention(slopes, obf, qt, vt, lam_vecs, g_diff_norm[l], lam_init, page_table, q_maps,
                              k_leaf[l].reshape(bs, ROWS, 128), v_leaf[l].reshape(bs, ROWS, 128),
                              cache_k2, cache_v2, l, FLASH_TQ, FLASH_TK, FLASH_SUB, DECODE_PAGES)
        dd_s = dd_s.reshape(bs, 2, N_HEADS, 128).transpose(0, 2, 1, 3).reshape(bs, GROUP)

        x = _dense_tail(r, dd, x, mods_p, l, w_out_bf, g_norm_ffn[l], wg_bf, wu_bf, wd_bf, TM_PROMPT, TF)
        xs = _dense_tail(r_s.reshape(bs, GROUP).astype(BF16), dd_s.astype(BF16), xs, mods_s,
                         l, w_out_bf, g_norm_ffn[l], wg_bf, wu_bf, wd_bf, bs, TF)

    def k_out(a, n):
        return a.reshape(depth, n, -1, N_HEADS, 2, D_MAP)

    def v_out(a, n):
        return (a.reshape(depth, n, -1, 2, N_HEADS, 128).transpose(0, 1, 2, 4, 3, 5)
                .reshape(depth, n, -1, N_HEADS, D_HEAD))

    return (x.reshape(1, t, d), xs.reshape(bs, 1, d), k_out(kv_p[0], 1), v_out(kv_p[1], 1), jnp.stack(s_p),
            k_out(kv_s[0], bs), v_out(kv_s[1], bs), s_s)
```

```python
import functools
import math

import jax
import jax.numpy as jnp
from jax import lax
from jax.experimental import pallas as pl
from jax.experimental.pallas import tpu as pltpu

F32 = jnp.float32
BF16 = jnp.bfloat16

D_MODEL = 2048
N_HEADS = 4
D_HEAD = 256
D_MAP = 128
GROUP = 1024
N_GROUPS = 7
D_FF = 5632
N_MOD = 6
EPS = 1e-6
PAGE = 128
ROWS = 2 * N_HEADS
VMEM_LIMIT = 56 * 1024 * 1024
LOG2E = math.log2(math.e)

TM_PROMPT = 512
TM_IN = 256
TF = 512
RET_CHUNK = 256
FLASH_TQ = 1024
FLASH_TK = 512
FLASH_SUB = 2
DECODE_PAGES = 16

NT_DIMS = (((1,), (1,)), ((), ()))
TN_DIMS = (((0,), (0,)), ((), ()))


def _silu(x):
    return x / (1.0 + jnp.exp(-x))


def _rms(x, g):
    return x * lax.rsqrt(jnp.mean(x * x, axis=-1, keepdims=True) + EPS) * g


def _params(*sem):
    return pltpu.CompilerParams(dimension_semantics=sem, vmem_limit_bytes=VMEM_LIMIT)


def _adaln_kernel(c_ref, w_ref, b_ref, o_ref):
    a = _silu(c_ref[...]).astype(BF16)
    o_ref[0] = jnp.dot(a, w_ref[0].astype(BF16), preferred_element_type=F32) + b_ref[0]


def _adaln(c_all, w_ada, b_ada, tn=1024):
    depth, d, n = w_ada.shape
    rows = c_all.shape[0]
    return pl.pallas_call(
        _adaln_kernel,
        grid=(depth, n // tn),
        in_specs=[pl.BlockSpec((rows, d), lambda l, j: (0, 0)),
                  pl.BlockSpec((1, d, tn), lambda l, j: (l, 0, j)),
                  pl.BlockSpec((1, 1, tn), lambda l, j: (l, 0, j))],
        out_specs=pl.BlockSpec((1, rows, tn), lambda l, j: (l, 0, j)),
        out_shape=jax.ShapeDtypeStruct((depth, rows, n), F32),
        compiler_params=_params("parallel", "parallel"),
        name="adaln",
    )(c_all, w_ada, b_ada.reshape(depth, 1, n))


def _mixer_in_kernel(*refs, q_scale, with_vt, aliased):
    x_ref, sh_ref, sc_ref, gn_ref, w_ref, gq_ref, gk_ref = refs[:7]
    obf_ref, g_ref, k_ref, v_ref = refs[7 + 2 * aliased:11 + 2 * aliased]
    vt_ref = refs[-1] if with_vt else None
    tm = x_ref.shape[0]

    y = _rms(x_ref[...], gn_ref[...])
    h = (y * (1.0 + sc_ref[...]) + sh_ref[...]).astype(BF16)

    def proj(j):
        return jnp.dot(h, w_ref[:, j * GROUP:(j + 1) * GROUP], preferred_element_type=F32)

    def maps(j):
        return [slice(j * GROUP + c * D_MAP, j * GROUP + (c + 1) * D_MAP) for c in range(GROUP // D_MAP)]

    obf_ref[:, 0:GROUP] = proj(0).astype(BF16)
    obf_ref[:, GROUP:2 * GROUP] = (proj(1) * (D_HEAD ** -0.5)).astype(BF16)
    obf_ref[:, 2 * GROUP:3 * GROUP] = proj(2).astype(BF16)
    g_ref[...] = proj(3)

    acc = proj(4)
    for c, dst in enumerate(maps(3)):
        obf_ref[:, dst] = (_rms(acc[:, c * D_MAP:(c + 1) * D_MAP], gq_ref[...]) * q_scale).astype(BF16)

    acc = proj(5)
    for c, dst in enumerate(maps(4)):
        kn = _rms(acc[:, c * D_MAP:(c + 1) * D_MAP], gk_ref[...])
        k_ref[pl.ds(c, tm, stride=ROWS), :] = kn
        obf_ref[:, dst] = kn.astype(BF16)

    acc = proj(6)
    for c in range(GROUP // D_MAP):
        head, half = divmod(c, 2)
        v_ref[pl.ds(half * N_HEADS + head, tm, stride=ROWS), :] = acc[:, c * D_MAP:(c + 1) * D_MAP]
    obf_ref[:, 5 * GROUP:6 * GROUP] = acc.astype(BF16)
    if with_vt:
        vt_ref[...] = acc.T.astype(BF16)


def _mixer_in(x, shift, scale, g_norm, w_bf, g_q, g_k, layer, depth, kv_leaves, tm, q_scale, with_vt=False):
    m, d = x.shape
    aliased = kv_leaves is not None
    mod_rows = shift.shape[0]
    mod_blk = (1, d) if mod_rows == 1 else (tm, d)
    mod_map = (lambda i: (0, 0)) if mod_rows == 1 else (lambda i: (i, 0))
    leaf_spec = pl.BlockSpec((None, tm * ROWS, 128), lambda i: (layer, i, 0))
    leaf_shape = jax.ShapeDtypeStruct((depth, m * ROWS, 128), F32)
    vt_spec = [pl.BlockSpec((GROUP, tm), lambda i: (0, i))] if with_vt else []
    vt_shape = [jax.ShapeDtypeStruct((GROUP, m), BF16)] if with_vt else []
    any_spec = pl.BlockSpec(memory_space=pl.ANY)
    return pl.pallas_call(
        functools.partial(_mixer_in_kernel, q_scale=q_scale, with_vt=with_vt, aliased=aliased),
        grid=(m // tm,),
        in_specs=[pl.BlockSpec((tm, d), lambda i: (i, 0)),
                  pl.BlockSpec(mod_blk, mod_map),
                  pl.BlockSpec(mod_blk, mod_map),
                  pl.BlockSpec((1, d), lambda i: (0, 0)),
                  pl.BlockSpec((None, d, N_GROUPS * GROUP), lambda i: (layer, 0, 0), pipeline_mode=pl.Buffered(1)),
                  pl.BlockSpec((1, D_MAP), lambda i: (0, 0)),
                  pl.BlockSpec((1, D_MAP), lambda i: (0, 0))] + [any_spec, any_spec] * aliased,
        out_specs=[pl.BlockSpec((tm, 6 * GROUP), lambda i: (i, 0)),
                   pl.BlockSpec((tm, GROUP), lambda i: (i, 0)), leaf_spec, leaf_spec] + vt_spec,
        out_shape=[jax.ShapeDtypeStruct((m, 6 * GROUP), BF16),
                   jax.ShapeDtypeStruct((m, GROUP), F32), leaf_shape, leaf_shape] + vt_shape,
        input_output_aliases={7: 2, 8: 3} if aliased else {},
        compiler_params=_params("parallel"),
        name="mixer_in",
    )(x, shift, scale, g_norm.reshape(1, d), w_bf, g_q.reshape(1, D_MAP), g_k.reshape(1, D_MAP),
      *(kv_leaves if aliased else ()))


def _ret_prompt_kernel(lg_ref, q_ref, k_ref, v_ref, g_ref, gn_ref, r_ref, s_ref, dec_scr, *, chunk):
    c = pl.program_id(0)
    row = lax.broadcasted_iota(jnp.int32, (chunk, chunk), 0)
    col = lax.broadcasted_iota(jnp.int32, (chunk, chunk), 1)
    pos = lax.broadcasted_iota(jnp.int32, (chunk, 1), 0).astype(F32)

    @pl.when(c == 0)
    def _():
        s_ref[...] = jnp.zeros_like(s_ref)
        diff = (row - col).astype(F32)
        for h in range(N_HEADS):
            dec_scr[h] = jnp.where(diff >= 0, jnp.exp(jnp.maximum(diff, 0.0) * lg_ref[h]), 0.0)

    for h in range(N_HEADS):
        lg = lg_ref[h]
        sl = slice(h * D_HEAD, (h + 1) * D_HEAD)
        q = q_ref[:, sl]
        k = k_ref[:, sl]
        v = v_ref[:, sl]
        q_decay = jnp.exp((pos + 1.0) * lg)
        k_decay = jnp.exp((chunk - 1.0 - pos) * lg)
        chunk_decay = jnp.exp(jnp.full((1, 1), chunk, F32) * lg)
        state = s_ref[h]

        scores = lax.dot_general(q, k, NT_DIMS, preferred_element_type=F32) * dec_scr[h]
        inner = jnp.dot(scores.astype(BF16), v, preferred_element_type=F32)
        cross = jnp.dot(q, state.astype(BF16), preferred_element_type=F32) * q_decay
        k_dec = (k.astype(F32) * k_decay).astype(BF16)
        s_ref[h] = chunk_decay * state + lax.dot_general(k_dec, v, TN_DIMS, preferred_element_type=F32)

        r_ref[:, sl] = (_rms(inner + cross, gn_ref[h]) * _silu(g_ref[:, sl])).astype(BF16)


def _ret_prompt(log_gamma, obf, g_r, g_ret_norm, chunk):
    t = obf.shape[0]
    return pl.pallas_call(
        functools.partial(_ret_prompt_kernel, chunk=chunk),
        grid=(t // chunk,),
        in_specs=[pl.BlockSpec(memory_space=pltpu.SMEM),
                  pl.BlockSpec((chunk, GROUP), lambda c: (c, 0)),
                  pl.BlockSpec((chunk, GROUP), lambda c: (c, 1)),
                  pl.BlockSpec((chunk, GROUP), lambda c: (c, 2)),
                  pl.BlockSpec((chunk, GROUP), lambda c: (c, 0)),
                  pl.BlockSpec((N_HEADS, 1, D_HEAD), lambda c: (0, 0, 0))],
        out_specs=[pl.BlockSpec((chunk, GROUP), lambda c: (c, 0)),
                   pl.BlockSpec((N_HEADS, D_HEAD, D_HEAD), lambda c: (0, 0, 0))],
        out_shape=[jax.ShapeDtypeStruct((t, GROUP), BF16),
                   jax.ShapeDtypeStruct((N_HEADS, D_HEAD, D_HEAD), F32)],
        scratch_shapes=[pltpu.VMEM((N_HEADS, chunk, chunk), F32)],
        compiler_params=_params("arbitrary"),
        name="ret_prompt",
    )(log_gamma, obf, obf, obf, g_r, g_ret_norm.reshape(N_HEADS, 1, D_HEAD))


def _lambda(lq1_ref, lk1_ref, lq2_ref, lk2_ref, lam_init):
    s1 = jnp.sum(lq1_ref[...] * lk1_ref[...], axis=-1, keepdims=True)
    s2 = jnp.sum(lq2_ref[...] * lk2_ref[...], axis=-1, keepdims=True)
    return jnp.exp(s1) - jnp.exp(s2) + lam_init


PAGE_ROWS = PAGE * ROWS
_V_ROW_SRC = tuple(2 * (j % N_HEADS) for j in range(ROWS))


def _attention_kernel(qi_tab, ki_tab, kpg_tab, vpg_tab, slope_ref, q_ref, k_ref, vt_ref, lq1_ref, lk1_ref, lq2_ref,
                      lk2_ref, gd_ref, qs_ref, kn_ref, vn_ref, slope_rows_ref, diag_ref, gd_rows_ref, *rest,
                      tq, tk, n_sub, lam_init, n_pages, past_len, n_seq):
    k_pages = rest[:n_pages]
    v_pages = rest[n_pages:2 * n_pages]
    o_ref, od_ref, m_scr, l_scr, acc_scr, logit_scr, w_scr, dacc_scr = rest[2 * n_pages:]
    h = pl.program_id(0)
    t = pl.program_id(1)

    n_chunks = past_len // (n_pages * PAGE)
    g = h * pl.num_programs(1) + t
    phase = g % n_chunks
    seq_k = g // n_chunks
    seq_v = seq_k - 1
    k_live = seq_k < n_seq
    v_live = jnp.logical_and(seq_v >= 0, seq_v < n_seq)
    lane = lax.broadcasted_iota(jnp.int32, (ROWS, 128), 1)

    def lane_sum(x):
        return jnp.dot(x.astype(BF16), jnp.ones((128, 128), BF16), preferred_element_type=F32)

    @pl.when(g == 0)
    def _():
        w_scr[...] = jnp.zeros_like(w_scr)
        dacc_scr[...] = jnp.zeros_like(dacc_scr)

    @pl.when(jnp.logical_and(v_live, phase == 0))
    def _():
        logits = logit_scr[...]
        e = jnp.exp(logits - jnp.max(logits, axis=-1, keepdims=True))
        a = e / jnp.sum(e, axis=-1, keepdims=True)
        lam = _lambda(lq1_ref, lk1_ref, lq2_ref, lk2_ref, lam_init)
        dd = a - lam * pltpu.roll(a, ROWS - 1, 0)
        row = lax.broadcasted_iota(jnp.int32, a.shape, 0)
        w = jnp.zeros_like(a)
        for j in range(ROWS):
            w = jnp.where(row == j, pltpu.roll(dd, (j - _V_ROW_SRC[j]) % ROWS, 0), w)
        w_scr[...] = w
        dacc_scr[...] = jnp.zeros_like(dacc_scr)

    @pl.when(jnp.logical_and(k_live, phase == 0))
    def _():
        l_new = jnp.sum(qs_ref[0] * kn_ref[0], axis=-1, keepdims=True)
        logit_scr[:, past_len:] = jnp.where(lane == 0, l_new, -jnp.inf)

    def score_k_chunk():
        q = qs_ref[0]
        diag = diag_ref[...]
        for i in range(n_pages):
            tok0 = pl.multiple_of((phase * n_pages + i) * PAGE, PAGE)
            prod = (k_pages[i][...].reshape(PAGE, ROWS, 128) * q[None]).reshape(PAGE_ROWS, 128)
            sums = lane_sum(prod).reshape(PAGE, ROWS, 128)
            logits = jnp.sum(sums * diag, axis=0)
            dist = (past_len - tok0 - lane).astype(F32)
            logit_scr[:, pl.ds(tok0, PAGE)] = logits - slope_rows_ref[...] * dist

    def apply_v_chunk():
        diag = diag_ref[...]
        acc = dacc_scr[...]
        for i in range(n_pages):
            tok0 = pl.multiple_of((phase * n_pages + i) * PAGE, PAGE)
            wc = w_scr[:, pl.ds(tok0, PAGE)]
            vp = v_pages[i][...].reshape(PAGE, ROWS, 128)
            for tkn in range(PAGE):
                acc = acc + jnp.broadcast_to(wc[:, tkn:tkn + 1], (ROWS, 128)) * vp[tkn]
        dacc_scr[...] = acc

    qi = qi_tab[t]
    ki = ki_tab[t]
    tkb = tk * n_sub
    ratio = tq // tkb
    slope = slope_ref[h] * LOG2E

    @pl.when(ki == 0)
    def _():
        m_scr[...] = jnp.full_like(m_scr, -jnp.inf)
        l_scr[...] = jnp.zeros_like(l_scr)
        acc_scr[...] = jnp.zeros_like(acc_scr)

    def tile(masked, sub):
        q = q_ref[...]
        k = k_ref[sub * tk:(sub + 1) * tk, :]
        vt = vt_ref[:, sub * tk:(sub + 1) * tk]
        koff = ki * tkb + sub * tk - qi * tq
        krow = lax.broadcasted_iota(jnp.int32, (tk, 128), 0)
        bias = pltpu.repeat(slope * (krow + koff).astype(F32), tq // 128, axis=1)
        if masked:
            kpos = lax.broadcasted_iota(jnp.int32, (tk, tq), 0) + koff
            keep = kpos <= lax.broadcasted_iota(jnp.int32, (tk, tq), 1)
        for m in range(2):
            sl = slice(m * D_MAP, (m + 1) * D_MAP)
            s = lax.dot_general(k[:, sl], q[:, sl], NT_DIMS, preferred_element_type=F32) + bias
            if masked:
                s = jnp.where(keep, s, -jnp.inf)
            m_old = m_scr[m]
            m_new = jnp.maximum(m_old, jnp.max(s, axis=0, keepdims=True))
            alpha = jnp.exp2(m_old - m_new)
            p = jnp.exp2(s - m_new)
            l_scr[m] = alpha * l_scr[m] + jnp.sum(p, axis=0, keepdims=True)
            acc_scr[m] = alpha * acc_scr[m] + jnp.dot(vt, p.astype(BF16), preferred_element_type=F32)
            m_scr[m] = m_new

    def step(masked):
        tile(masked, 0)
        score_k_chunk()
        for sub in range(1, n_sub):
            tile(masked, sub)
        apply_v_chunk()

    @pl.when(ki < qi * ratio)
    def _():
        step(False)

    @pl.when(ki >= qi * ratio)
    def _():
        step(True)

    @pl.when(ki == (qi + 1) * ratio - 1)
    def _():
        lam = _lambda(lq1_ref, lk1_ref, lq2_ref, lk2_ref, lam_init)
        o = acc_scr[0] * (1.0 / l_scr[0]) - lam * (acc_scr[1] * (1.0 / l_scr[1]))
        y = o * lax.rsqrt(jnp.mean(o * o, axis=0, keepdims=True) + EPS)
        y = y * pltpu.repeat(gd_ref[...], tq // 128, axis=1) * (1.0 - lam_init)
        o_ref[...] = y.T.astype(BF16)

    @pl.when(jnp.logical_and(v_live, phase == n_chunks - 1))
    def _():
        out = dacc_scr[...] + w_scr[:, past_len:][:, :1] * vn_ref[0]
        sq = jnp.sum(out * out, axis=-1, keepdims=True)
        ms = (sq + pltpu.roll(sq, N_HEADS, 0)) / D_HEAD
        od_ref[0] = out * lax.rsqrt(ms + EPS) * gd_rows_ref[...] * (1.0 - lam_init)


def _attention(slopes, obf, vt, lam_vecs, g_diff_norm, lam_init, page_table, q_maps, k_new, v_new,
               cache_k, cache_v, layer, tq, tk, n_sub, n_pages):
    t = obf.shape[0]
    tkb = tk * n_sub
    ratio = tq // tkb
    pairs = [(qi, ki) for qi in range(t // tq) for ki in range((qi + 1) * ratio)]
    n_steps = len(pairs)
    qi_tab = jnp.asarray([p[0] for p in pairs], jnp.int32)
    ki_tab = jnp.asarray([p[1] for p in pairs], jnp.int32)
    q_blk, k_blk = 3 * N_HEADS, 4 * N_HEADS
    gd_rep = jnp.broadcast_to(g_diff_norm.reshape(D_HEAD, 1), (D_HEAD, 128))

    n_seq, pages_per_seq = page_table.shape
    past_len = pages_per_seq * PAGE
    n_chunks = pages_per_seq // n_pages
    last = n_seq * n_chunks - 1
    assert N_HEADS * n_steps >= (n_seq + 1) * n_chunks, "not enough grid steps to stream the cache"
    slope_rows = jnp.broadcast_to(jnp.repeat(slopes, 2)[:, None], (ROWS, 128))
    diag = jnp.broadcast_to(jnp.eye(PAGE, 128, dtype=F32)[:, None, :], (PAGE, ROWS, 128))
    gd_rows = jnp.repeat(g_diff_norm.reshape(2, 128), N_HEADS, axis=0)

    def step(h, t):
        return h * n_steps + t

    g_all = jnp.arange(N_HEADS * n_steps)
    chunk_pages = page_table.reshape(n_seq * n_chunks, n_pages)
    kpg_tab = chunk_pages[jnp.minimum(g_all, last)].reshape(-1)
    vpg_tab = chunk_pages[jnp.clip(g_all - n_chunks, 0, last)].reshape(-1)

    def k_spec(i):
        return pl.BlockSpec((None, None, PAGE_ROWS, 128),
                            lambda h, t, qt, kt, kp, vp: (layer, kp[step(h, t) * n_pages + i], 0, 0))

    def v_spec(i):
        return pl.BlockSpec((None, None, PAGE_ROWS, 128),
                            lambda h, t, qt, kt, kp, vp: (layer, vp[step(h, t) * n_pages + i], 0, 0))

    vec = pl.BlockSpec((1, D_MAP), lambda h, t, qt, kt, kp, vp: (0, 0))
    tok_k = pl.BlockSpec((1, ROWS, 128),
                         lambda h, t, qt, kt, kp, vp: (jnp.minimum(step(h, t) // n_chunks, n_seq - 1), 0, 0))
    tok_v = pl.BlockSpec((1, ROWS, 128),
                         lambda h, t, qt, kt, kp, vp: (jnp.clip(step(h, t) // n_chunks - 1, 0, n_seq - 1), 0, 0))
    const = pl.BlockSpec((ROWS, 128), lambda h, t, qt, kt, kp, vp: (0, 0))
    grid_spec = pltpu.PrefetchScalarGridSpec(
        num_scalar_prefetch=4,
        grid=(N_HEADS, n_steps),
        in_specs=[pl.BlockSpec(memory_space=pltpu.SMEM),
                  pl.BlockSpec((tq, D_HEAD), lambda h, t, qt, kt, kp, vp: (qt[t], q_blk + h)),
                  pl.BlockSpec((tkb, D_HEAD), lambda h, t, qt, kt, kp, vp: (kt[t], k_blk + h)),
                  pl.BlockSpec((D_HEAD, tkb), lambda h, t, qt, kt, kp, vp: (h, kt[t])),
                  vec, vec, vec, vec,
                  pl.BlockSpec((D_HEAD, 128), lambda h, t, qt, kt, kp, vp: (0, 0)),
                  tok_k, tok_k, tok_v, const,
                  pl.BlockSpec((PAGE, ROWS, 128), lambda h, t, qt, kt, kp, vp: (0, 0, 0)),
                  const]
                 + [k_spec(i) for i in range(n_pages)] + [v_spec(i) for i in range(n_pages)],
        out_specs=[pl.BlockSpec((tq, D_HEAD), lambda h, t, qt, kt, kp, vp: (qt[t], h)), tok_v],
        scratch_shapes=[pltpu.VMEM((2, 1, tq), F32),
                        pltpu.VMEM((2, 1, tq), F32),
                        pltpu.VMEM((2, D_HEAD, tq), F32),
                        pltpu.VMEM((ROWS, past_len + 128), F32),
                        pltpu.VMEM((ROWS, past_len + 128), F32),
                        pltpu.VMEM((ROWS, 128), F32)],
    )
    return pl.pallas_call(
        functools.partial(_attention_kernel, tq=tq, tk=tk, n_sub=n_sub, lam_init=lam_init, n_pages=n_pages,
                          past_len=past_len, n_seq=n_seq),
        grid_spec=grid_spec,
        out_shape=[jax.ShapeDtypeStruct((t, GROUP), BF16), jax.ShapeDtypeStruct((n_seq, ROWS, 128), F32)],
        compiler_params=_params("arbitrary", "arbitrary"),
        name="attention",
    )(qi_tab, ki_tab, kpg_tab, vpg_tab, slopes, obf, obf, vt, *lam_vecs, gd_rep, q_maps, k_new, v_new, slope_rows,
      diag, gd_rows, *([cache_k] * n_pages), *([cache_v] * n_pages))


def _mixer_out_kernel(r_ref, d_ref, wr_ref, wd_ref, x_ref, gt_ref, sh_ref, sc_ref, gn_ref, xo_ref, h_ref):
    o = (jnp.dot(r_ref[...], wr_ref[...], preferred_element_type=F32)
         + jnp.dot(d_ref[...], wd_ref[...], preferred_element_type=F32))
    x = x_ref[...] + gt_ref[...] * o
    xo_ref[...] = x
    h_ref[...] = (_rms(x, gn_ref[...]) * (1.0 + sc_ref[...]) + sh_ref[...]).astype(BF16)


def _mixer_out(r, dd, w_out_bf, layer, x, gate, shift, scale, g_norm, tm):
    m, d = x.shape
    mod_rows = gate.shape[0]
    mod = pl.BlockSpec((1, d), lambda i: (0, 0)) if mod_rows == 1 else pl.BlockSpec((tm, d), lambda i: (i, 0))
    return pl.pallas_call(
        _mixer_out_kernel,
        grid=(m // tm,),
        in_specs=[pl.BlockSpec((tm, GROUP), lambda i: (i, 0)),
                  pl.BlockSpec((tm, GROUP), lambda i: (i, 0)),
                  pl.BlockSpec((None, GROUP, d), lambda i: (layer, 0, 0)),
                  pl.BlockSpec((None, GROUP, d), lambda i: (layer, 1, 0)),
                  pl.BlockSpec((tm, d), lambda i: (i, 0)),
                  mod, mod, mod,
                  pl.BlockSpec((1, d), lambda i: (0, 0))],
        out_specs=[pl.BlockSpec((tm, d), lambda i: (i, 0)),
                   pl.BlockSpec((tm, d), lambda i: (i, 0))],
        out_shape=[jax.ShapeDtypeStruct((m, d), F32), jax.ShapeDtypeStruct((m, d), BF16)],
        compiler_params=_params("parallel"),
        name="mixer_out",
    )(r, dd, w_out_bf, w_out_bf, x, gate, shift, scale, g_norm.reshape(1, d))


def _ffn_kernel(h_ref, wg_ref, wu_ref, wd_ref, x_ref, gt_ref, o_ref, acc_scr):
    f = pl.program_id(1)

    @pl.when(f == 0)
    def _():
        acc_scr[...] = jnp.zeros_like(acc_scr)

    h = h_ref[...]
    half = wg_ref.shape[1] // 2
    part = None
    for c in range(2):
        sl = slice(c * half, (c + 1) * half)
        a = (_silu(jnp.dot(h, wg_ref[:, sl], preferred_element_type=F32))
             * jnp.dot(h, wu_ref[:, sl], preferred_element_type=F32)).astype(BF16)
        p = jnp.dot(a, wd_ref[sl, :], preferred_element_type=F32)
        part = p if part is None else part + p
    acc_scr[...] += part

    @pl.when(f == pl.num_programs(1) - 1)
    def _():
        o_ref[...] = x_ref[...] + gt_ref[...] * acc_scr[...]


def _ffn(h, wg_bf, wu_bf, wd_bf, layer, x, gate, tm, tf):
    m, d = x.shape
    ff = wg_bf.shape[2]
    mod_rows = gate.shape[0]
    mod = (pl.BlockSpec((1, d), lambda i, f: (0, 0)) if mod_rows == 1
           else pl.BlockSpec((tm, d), lambda i, f: (i, 0)))
    return pl.pallas_call(
        _ffn_kernel,
        grid=(m // tm, ff // tf),
        in_specs=[pl.BlockSpec((tm, d), lambda i, f: (i, 0)),
                  pl.BlockSpec((None, d, tf), lambda i, f: (layer, 0, f)),
                  pl.BlockSpec((None, d, tf), lambda i, f: (layer, 0, f)),
                  pl.BlockSpec((None, tf, d), lambda i, f: (layer, f, 0)),
                  pl.BlockSpec((tm, d), lambda i, f: (i, 0)),
                  mod],
        out_specs=pl.BlockSpec((tm, d), lambda i, f: (i, 0)),
        out_shape=jax.ShapeDtypeStruct((m, d), F32),
        scratch_shapes=[pltpu.VMEM((tm, d), F32)],
        compiler_params=_params("parallel", "arbitrary"),
        name="ffn",
    )(h, wg_bf, wu_bf, wd_bf, x, gate)


def _ret_sample_kernel(*refs, aliased):
    lg_ref, s_ref, q_ref, k_ref, v_ref, g_ref, gn_ref = refs[:7]
    so_ref, r_ref = refs[7 + aliased:]
    ones = jnp.ones((D_HEAD, 128), BF16)
    eye = (lax.broadcasted_iota(jnp.int32, (D_HEAD, D_HEAD), 0)
           == lax.broadcasted_iota(jnp.int32, (D_HEAD, D_HEAD), 1))

    def column(row):
        rep = jnp.dot(jnp.where(eye, row, 0.0).astype(BF16), ones, preferred_element_type=F32)
        return pltpu.repeat(rep, D_HEAD // 128, axis=1)

    for h in range(N_HEADS):
        gamma = jnp.exp(jnp.full((1, 1), lg_ref[h], F32))
        state = s_ref[0, h]
        q = q_ref[0, h]
        k = k_ref[0, h]
        vr = v_ref[0, h]
        so_ref[0, h] = gamma * state + column(k) * vr
        qk = jnp.sum(q * k, axis=-1, keepdims=True)
        o = qk * vr + gamma * jnp.sum(column(q) * state, axis=0, keepdims=True)
        r_ref[0, h] = _rms(o, gn_ref[h]) * _silu(g_ref[0, h])


def _ret_sample(log_gamma, state_all, layer, q_row, k_row, v_row, g_row, g_ret_norm, s_prev):
    depth, b = state_all.shape[:2]
    aliased = s_prev is not None
    row = pl.BlockSpec((1, N_HEADS, 1, D_HEAD), lambda i: (i, 0, 0, 0))
    st = pl.BlockSpec((None, 1, N_HEADS, D_HEAD, D_HEAD), lambda i: (layer, i, 0, 0, 0))
    return pl.pallas_call(
        functools.partial(_ret_sample_kernel, aliased=aliased),
        grid=(b,),
        in_specs=[pl.BlockSpec(memory_space=pltpu.SMEM), st, row, row, row, row,
                  pl.BlockSpec((N_HEADS, 1, D_HEAD), lambda i: (0, 0, 0))]
                 + [pl.BlockSpec(memory_space=pl.ANY)] * aliased,
        out_specs=[st, row],
        out_shape=[jax.ShapeDtypeStruct(state_all.shape, F32),
                   jax.ShapeDtypeStruct((b, N_HEADS, 1, D_HEAD), F32)],
        input_output_aliases={7: 0} if aliased else {},
        compiler_params=_params("parallel"),
        name="ret_sample",
    )(log_gamma, state_all, q_row, k_row, v_row, g_row, g_ret_norm.reshape(N_HEADS, 1, D_HEAD),
      *((s_prev,) if aliased else ()))


def _dense_tail(r, dd, x, mods, layer, w_out_bf, g_norm_ffn, wg_bf, wu_bf, wd_bf, tm, tf):
    _, _, gt_a, sh_f, sc_f, gt_f = mods
    x, h = _mixer_out(r, dd, w_out_bf, layer, x, gt_a, sh_f, sc_f, g_norm_ffn, tm)
    return _ffn(h, wg_bf, wu_bf, wd_bf, layer, x, gt_f, tm, tf)


def kernel(x_prompt, x_sample, cache_k, cache_v, state_ret, page_table, c_prompt, c_sample,
           w_ada, b_ada, g_norm_mix, w_in, g_qnorm, g_knorm, lambda_q1, lambda_k1, lambda_q2,
           lambda_k2, g_ret_norm, g_diff_norm, w_out, g_norm_ffn, w_gate, w_up, w_down):
    depth = w_in.shape[0]
    n_prompt, t, d = x_prompt.shape
    assert n_prompt == 1
    bs = x_sample.shape[0]
    n_pool = cache_k.shape[1]

    log_gamma = jnp.log1p(-jnp.exp2(-5.0 - jnp.arange(N_HEADS, dtype=F32)))
    slopes = jnp.exp2(-8.0 * (jnp.arange(N_HEADS, dtype=F32) + 1.0) / N_HEADS)
    q_scale = D_MAP ** -0.5

    pad = (-(bs + 1)) % 8
    c_all = jnp.concatenate([c_sample, c_prompt, jnp.zeros((pad, d), F32)], axis=0)
    mod = _adaln(c_all, w_ada, b_ada)

    w_in_bf, w_out_bf = w_in.astype(BF16), w_out.astype(BF16)
    wg_bf, wu_bf, wd_bf = w_gate.astype(BF16), w_up.astype(BF16), w_down.astype(BF16)
    cache_k2 = cache_k.reshape(depth, n_pool, PAGE_ROWS, 128)
    cache_v2 = (cache_v.reshape(depth, n_pool, PAGE, N_HEADS, 2, 128).transpose(0, 1, 2, 4, 3, 5)
                .reshape(depth, n_pool, PAGE_ROWS, 128))

    x = x_prompt.reshape(t, d)
    xs = x_sample.reshape(bs, d)
    kv_p = kv_s = s_s = None
    s_p = []
    for l in range(depth):
        lam_init = 0.8 - 0.6 * math.exp(-0.3 * l)
        lam_vecs = [a[l].reshape(1, D_MAP) for a in (lambda_q1, lambda_k1, lambda_q2, lambda_k2)]
        mods_p = [mod[l, bs:bs + 1, i * d:(i + 1) * d] for i in range(N_MOD)]
        mods_s = [mod[l, :bs, i * d:(i + 1) * d] for i in range(N_MOD)]

        obf, g_r, k_leaf, v_leaf, vt = _mixer_in(x, mods_p[0], mods_p[1], g_norm_mix[l], w_in_bf, g_qnorm[l],
                                                 g_knorm[l], l, depth, kv_p, TM_IN, q_scale * LOG2E, with_vt=True)
        kv_p = (k_leaf, v_leaf)
        obf_s, g_s, k_leaf, v_leaf = _mixer_in(xs, mods_s[0], mods_s[1], g_norm_mix[l], w_in_bf, g_qnorm[l],
                                               g_knorm[l], l, depth, kv_s, bs, q_scale)
        kv_s = (k_leaf, v_leaf)

        r, s_fin = _ret_prompt(log_gamma, obf, g_r, g_ret_norm[l], RET_CHUNK)
        s_p.append(s_fin.reshape(1, N_HEADS, D_HEAD, D_HEAD))

        def rows(c):
            return obf_s[:, c * GROUP:(c + 1) * GROUP].astype(F32).reshape(bs, N_HEADS, 1, D_HEAD)

        s_s, r_s = _ret_sample(log_gamma, state_ret, l, rows(0), rows(1), rows(2),
                               g_s.reshape(bs, N_HEADS, 1, D_HEAD), g_ret_norm[l], s_s)
        q_maps = obf_s[:, 3 * GROUP:4 * GROUP].astype(F32).reshape(bs, ROWS, D_MAP)
        dd, dd_s = _attention(slopes, obf, vt, lam_vecs, g_diff_norm[l], lam_init, page_table, q_maps,
                              k_leaf[l].reshape(bs, ROWS, 128), v_leaf[l].reshape(bs, ROWS, 128),
                              cache_k2, cache_v2, l, FLASH_TQ, FLASH_TK, FLASH_SUB, DECODE_PAGES)
        dd_s = dd_s.reshape(bs, 2, N_HEADS, 128).transpose(0, 2, 1, 3).reshape(bs, GROUP)

        x = _dense_tail(r, dd, x, mods_p, l, w_out_bf, g_norm_ffn[l], wg_bf, wu_bf, wd_bf, TM_PROMPT, TF)
        xs = _dense_tail(r_s.reshape(bs, GROUP).astype(BF16), dd_s.astype(BF16), xs, mods_s,
                         l, w_out_bf, g_norm_ffn[l], wg_bf, wu_bf, wd_bf, bs, TF)

    def k_out(a, n):
        return a.reshape(depth, n, -1, N_HEADS, 2, D_MAP)

    def v_out(a, n):
        return (a.reshape(depth, n, -1, 2, N_HEADS, 128).transpose(0, 1, 2, 4, 3, 5)
                .reshape(depth, n, -1, N_HEADS, D_HEAD))

    return (x.reshape(1, t, d), xs.reshape(bs, 1, d), k_out(kv_p[0], 1), v_out(kv_p[1], 1), jnp.stack(s_p),
            k_out(kv_s[0], bs), v_out(kv_s[1], bs), s_s)
```
